```python
import math
import jax
import jax.numpy as jnp
from jax import lax
import numpy as np


D_MODEL = 1024
BATCH = 4
SEQ = 8192
DEPTH = 1

S5_WIDTH = D_MODEL // 2
S5_GROUP = 16
S5_GROUPS = S5_WIDTH // S5_GROUP
S5_STATE = 64
S5_DT_MIN = 0.001
S5_DT_MAX = 0.1
SSD_WIDTH = D_MODEL
SSD_HEADDIM = 64
SSD_HEADS = SSD_WIDTH // SSD_HEADDIM
SSD_GROUPS = 4
SSD_STATE = 64
SSD_CONV = 5
SSD_CHUNK = 128
SSD_CONV_DIM = SSD_WIDTH + 2 * SSD_GROUPS * SSD_STATE
PEER_HEADS = 8
PEER_NKEYS = 128
PEER_EXPERTS = PEER_NKEYS * PEER_NKEYS
PEER_DKEY = 256
PEER_TOPK = 16
PEER_BLOCK = 128
IN_WIDTH = S5_WIDTH + SSD_WIDTH + SSD_CONV_DIM + 2 * SSD_HEADS + 2 * D_MODEL
IN_SPLITS = (S5_WIDTH,
             S5_WIDTH + SSD_WIDTH,
             S5_WIDTH + SSD_WIDTH + SSD_CONV_DIM,
             S5_WIDTH + SSD_WIDTH + SSD_CONV_DIM + 2 * SSD_HEADS,
             S5_WIDTH + SSD_WIDTH + SSD_CONV_DIM + 2 * SSD_HEADS + D_MODEL)
RMS_EPS = 1e-6

kernel_name = 'hybrid_s5_ssd_peer_encoder'


def rmsnorm(x, w):
    xf = x.astype(jnp.float32)
    return xf * lax.rsqrt(jnp.mean(xf * xf, axis=-1, keepdims=True) + RMS_EPS) * w.astype(jnp.float32)


def _cplx_combine(e_i, e_j):
    ar_i, ai_i, br_i, bi_i = e_i
    ar_j, ai_j, br_j, bi_j = e_j
    return (ar_j * ar_i - ai_j * ai_i,
            ar_j * ai_i + ai_j * ar_i,
            ar_j * br_i - ai_j * bi_i + br_j,
            ar_j * bi_i + ai_j * br_i + bi_j)


def s5_bidirectional(u, a_re, a_im, log_dt, b_re, b_im, c_re, c_im, d_skip):
    bsz, seqlen, _ = u.shape
    ug = u.reshape(bsz, seqlen, S5_GROUPS, S5_GROUP)
    y = u * d_skip
    for direction in range(2):
        ar = a_re[direction].astype(jnp.float32)
        ai = a_im[direction].astype(jnp.float32)
        step = jnp.exp(log_dt[direction].astype(jnp.float32))[:, None]
        mag = jnp.exp(ar * step)
        lam_r = mag * jnp.cos(ai * step)
        lam_i = mag * jnp.sin(ai * step)
        den = ar * ar + ai * ai
        f_r = ((lam_r - 1.0) * ar + lam_i * ai) / den
        f_i = (lam_i * ar - (lam_r - 1.0) * ai) / den
        bu_r0 = jnp.einsum('blgc,gpc->blgp', ug, b_re[direction])
        bu_i0 = jnp.einsum('blgc,gpc->blgp', ug, b_im[direction])
        bu_r = f_r * bu_r0 - f_i * bu_i0
        bu_i = f_r * bu_i0 + f_i * bu_r0
        lr = jnp.broadcast_to(lam_r, (1, seqlen) + lam_r.shape)
        li = jnp.broadcast_to(lam_i, (1, seqlen) + lam_i.shape)
        _, _, s_r, s_i = lax.associative_scan(_cplx_combine, (lr, li, bu_r, bu_i),
                                              reverse=(direction == 1), axis=1)
        y_dir = (jnp.einsum('blgp,gcp->blgc', s_r, c_re[direction])
                 - jnp.einsum('blgp,gcp->blgc', s_i, c_im[direction]))
        y = y + y_dir.reshape(bsz, seqlen, S5_WIDTH)
    return y


def _segsum_exp(cs):
    t = cs.shape[-1]
    diff = cs[..., :, None] - cs[..., None, :]
    mask = jnp.tril(jnp.ones((t, t), dtype=bool))
    return jnp.exp(jnp.where(mask, diff, -jnp.inf))


def ssd_scan(xh, dt, a, bg, cg):
    bsz, seqlen = xh.shape[:2]
    nc = seqlen // SSD_CHUNK
    hg = SSD_HEADS // SSD_GROUPS
    xdt = (xh * dt[..., None]).reshape(bsz, nc, SSD_CHUNK, SSD_GROUPS, hg, SSD_HEADDIM)
    a_dt = jnp.moveaxis((dt * a).reshape(bsz, nc, SSD_CHUNK, SSD_GROUPS, hg), 2, -1)
    a_cs = jnp.cumsum(a_dt, axis=-1)
    bc = bg.reshape(bsz, nc, SSD_CHUNK, SSD_GROUPS, SSD_STATE)
    cc = cg.reshape(bsz, nc, SSD_CHUNK, SSD_GROUPS, SSD_STATE)
    decay_in = _segsum_exp(a_cs)
    cb = jnp.einsum('bclgn,bcsgn->bcgls', cc, bc)
    y_diag = jnp.einsum('bcghls,bcsghp->bclghp', cb[:, :, :, None] * decay_in, xdt)
    decay_states = jnp.exp(a_cs[..., -1:] - a_cs)
    states = jnp.einsum('bclgn,bcghl,bclghp->bcghpn', bc, decay_states, xdt)
    chunk_tot = jnp.moveaxis(a_cs[..., -1], 1, -1)
    chunk_cs = jnp.cumsum(jnp.pad(chunk_tot, ((0, 0), (0, 0), (0, 0), (1, 0))), axis=-1)
    decay_chunk = _segsum_exp(chunk_cs)
    states_p = jnp.concatenate([jnp.zeros_like(states[:, :1]), states], axis=1)
    new_states = jnp.einsum('bghzc,bcghpn->bzghpn', decay_chunk, states_p)
    prev_states = new_states[:, :-1]
    y_off = jnp.einsum('bclgn,bcghpn,bcghl->bclghp', cc, prev_states, jnp.exp(a_cs))
    return (y_diag + y_off).reshape(bsz, seqlen, SSD_HEADS, SSD_HEADDIM)


def ssd_bidirectional(z, xbc, dt_raw, conv_w, conv_b, dt_bias, a_log, d_skip, norm_w, w_out):
    bsz, seqlen, _ = xbc.shape
    pad = SSD_CONV // 2
    xbc = lax.conv_general_dilated(xbc, conv_w.astype(xbc.dtype), window_strides=(1,),
                                   padding=[(pad, pad)],
                                   dimension_numbers=('NWC', 'WIO', 'NWC'),
                                   feature_group_count=SSD_CONV_DIM)
    xbc = jax.nn.silu(xbc + conv_b)
    xs, bs, cs = jnp.split(xbc, (SSD_WIDTH, SSD_WIDTH + SSD_GROUPS * SSD_STATE), axis=-1)
    xh = xs.reshape(bsz, seqlen, SSD_HEADS, SSD_HEADDIM)
    bg = bs.reshape(bsz, seqlen, SSD_GROUPS, SSD_STATE)
    cg = cs.reshape(bsz, seqlen, SSD_GROUPS, SSD_STATE)
    y = xh * d_skip[:, None]
    for direction in range(2):
        dt = jax.nn.softplus(dt_raw[..., direction * SSD_HEADS:(direction + 1) * SSD_HEADS].astype(jnp.float32)
                             + dt_bias[direction])
        a = -jnp.exp(a_log[direction].astype(jnp.float32))
        if direction == 0:
            y = y + ssd_scan(xh, dt, a, bg, cg)
        else:
            y_b = ssd_scan(jnp.flip(xh, 1), jnp.flip(dt, 1), a, jnp.flip(bg, 1), jnp.flip(cg, 1))
            y = y + jnp.flip(y_b, 1)
    y = y.reshape(bsz, seqlen, SSD_WIDTH) * jax.nn.silu(z)
    yg = y.reshape(bsz, seqlen, SSD_GROUPS, SSD_WIDTH // SSD_GROUPS).astype(jnp.float32)
    yg = yg * lax.rsqrt(jnp.mean(yg * yg, axis=-1, keepdims=True) + RMS_EPS)
    y = yg.reshape(bsz, seqlen, SSD_WIDTH) * norm_w
    return y @ w_out


def peer_ffn(xn, w_q, sub_keys, u_tab, v_tab):
    bsz, seqlen, d = xn.shape
    t = bsz * seqlen
    xt = xn.reshape(t, d)
    q = (xt @ w_q).reshape(t, PEER_HEADS, 2, PEER_DKEY // 2).astype(jnp.float32)
    scores = jnp.einsum('thsd,hskd->thsk', q, sub_keys.astype(jnp.float32))
    top_s, top_i = lax.top_k(scores, PEER_TOPK)
    cand = (top_s[:, :, 0, :, None] + top_s[:, :, 1, None, :]).reshape(t, PEER_HEADS, PEER_TOPK * PEER_TOPK)
    best_s, best_c = lax.top_k(cand, PEER_TOPK)
    i1 = jnp.take_along_axis(top_i[:, :, 0], best_c // PEER_TOPK, axis=-1)
    i2 = jnp.take_along_axis(top_i[:, :, 1], best_c % PEER_TOPK, axis=-1)
    expert = (i1 * PEER_NKEYS + i2).reshape(t, PEER_HEADS * PEER_TOPK)
    gates = jax.nn.softmax(best_s, axis=-1).reshape(t, PEER_HEADS * PEER_TOPK)
    nb = t // PEER_BLOCK

    def block(args):
        xb, eb, gb = args
        hid = jnp.einsum('td,tkd->tk', xb, u_tab[eb])
        act = jax.nn.gelu(hid, approximate=False) * gb
        return jnp.einsum('tk,tkd->td', act, v_tab[eb])

    out = lax.map(block, (xt.reshape(nb, PEER_BLOCK, d),
                          expert.reshape(nb, PEER_BLOCK, PEER_HEADS * PEER_TOPK),
                          gates.reshape(nb, PEER_BLOCK, PEER_HEADS * PEER_TOPK)))
    return out.reshape(bsz, seqlen, d)


def setup_inputs(seed: int = 0) -> dict:
    key = jax.random.key(seed)
    ks = jax.random.split(key, 26)
    f32 = jnp.float32

    def nrm(k, shape, scale):
        return jax.random.normal(k, shape, f32) * scale

    def gain(k, shape):
        return 1.0 + 0.01 * jax.random.normal(k, shape, f32)

    nl = DEPTH
    x = nrm(ks[0], (BATCH, SEQ, D_MODEL), 1.0)
    mix_norm_w = gain(ks[1], (nl, D_MODEL))
    w_in = nrm(ks[2], (nl, D_MODEL, IN_WIDTH), D_MODEL ** -0.5)
    s5_a_re = -0.5 + nrm(ks[3], (nl, 2, S5_GROUPS, S5_STATE), 0.01)
    s5_a_im = math.pi * jnp.arange(S5_STATE, dtype=f32) + nrm(ks[4], (nl, 2, S5_GROUPS, S5_STATE), 0.01)
    s5_log_dt = jax.random.uniform(ks[5], (nl, 2, S5_GROUPS), f32, math.log(S5_DT_MIN), math.log(S5_DT_MAX))
    s5_b_re = nrm(ks[6], (nl, 2, S5_GROUPS, S5_STATE, S5_GROUP), (2 * S5_GROUP) ** -0.5)
    s5_b_im = nrm(ks[7], (nl, 2, S5_GROUPS, S5_STATE, S5_GROUP), (2 * S5_GROUP) ** -0.5)
    s5_c_re = nrm(ks[8], (nl, 2, S5_GROUPS, S5_GROUP, S5_STATE), S5_STATE ** -0.5)
    s5_c_im = nrm(ks[9], (nl, 2, S5_GROUPS, S5_GROUP, S5_STATE), S5_STATE ** -0.5)
    s5_d = nrm(ks[10], (nl, S5_WIDTH), 1.0)
    w_glu = nrm(ks[11], (nl, S5_WIDTH, 2 * D_MODEL), S5_WIDTH ** -0.5)
    conv_w = nrm(ks[12], (nl, SSD_CONV, 1, SSD_CONV_DIM), SSD_CONV ** -0.5)
    conv_b = nrm(ks[13], (nl, SSD_CONV_DIM), 0.02)
    dt0 = jnp.exp(jax.random.uniform(ks[14], (nl, 2, SSD_HEADS), f32, math.log(1e-3), math.log(1e-1)))
    ssd_dt_bias = dt0 + jnp.log(-jnp.expm1(-dt0))
    ssd_a_log = jnp.log(jax.random.uniform(ks[15], (nl, 2, SSD_HEADS), f32, 1.0, 16.0))
    ssd_d = gain(ks[16], (nl, SSD_HEADS))
    ssd_norm_w = gain(ks[17], (nl, SSD_WIDTH))
    w_ssd_out = nrm(ks[18], (nl, SSD_WIDTH, D_MODEL), SSD_WIDTH ** -0.5)
    w_o = nrm(ks[19], (nl, D_MODEL, D_MODEL), D_MODEL ** -0.5)
    ffn_norm_w = gain(ks[20], (nl, D_MODEL))
    peer_w_q = nrm(ks[21], (nl, D_MODEL, PEER_HEADS * PEER_DKEY), D_MODEL ** -0.5)
    peer_sub_keys = nrm(ks[22], (nl, PEER_HEADS, 2, PEER_NKEYS, PEER_DKEY // 2), (PEER_DKEY // 2) ** -0.5)
    peer_u = nrm(ks[23], (nl, PEER_EXPERTS, D_MODEL), D_MODEL ** -0.5)
    peer_v = nrm(ks[24], (nl, PEER_EXPERTS, D_MODEL), PEER_HEADS ** -0.5)
    final_norm_w = gain(ks[25], (D_MODEL,))
    return {'x': x, 'mix_norm_w': mix_norm_w, 'w_in': w_in,
            's5_a_re': s5_a_re, 's5_a_im': s5_a_im, 's5_log_dt': s5_log_dt,
            's5_b_re': s5_b_re, 's5_b_im': s5_b_im, 's5_c_re': s5_c_re, 's5_c_im': s5_c_im,
            's5_d': s5_d, 'w_glu': w_glu, 'conv_w': conv_w, 'conv_b': conv_b,
            'ssd_dt_bias': ssd_dt_bias, 'ssd_a_log': ssd_a_log, 'ssd_d': ssd_d,
            'ssd_norm_w': ssd_norm_w, 'w_ssd_out': w_ssd_out, 'w_o': w_o,
            'ffn_norm_w': ffn_norm_w, 'peer_w_q': peer_w_q, 'peer_sub_keys': peer_sub_keys,
            'peer_u': peer_u, 'peer_v': peer_v, 'final_norm_w': final_norm_w}


def reference(x, mix_norm_w, w_in, s5_a_re, s5_a_im, s5_log_dt, s5_b_re, s5_b_im, s5_c_re, s5_c_im,
              s5_d, w_glu, conv_w, conv_b, ssd_dt_bias, ssd_a_log, ssd_d, ssd_norm_w, w_ssd_out, w_o,
              ffn_norm_w, peer_w_q, peer_sub_keys, peer_u, peer_v, final_norm_w):
    h = x
    for layer in range(DEPTH):
        xn = rmsnorm(h, mix_norm_w[layer])
        proj = xn @ w_in[layer]
        u_s5, z, xbc, dt_raw, gate_a, gate_b = jnp.split(proj, IN_SPLITS, axis=-1)
        y_a = s5_bidirectional(u_s5, s5_a_re[layer], s5_a_im[layer], s5_log_dt[layer],
                               s5_b_re[layer], s5_b_im[layer], s5_c_re[layer], s5_c_im[layer],
                               s5_d[layer])
        glu = jax.nn.gelu(y_a, approximate=False) @ w_glu[layer]
        y_a = glu[..., :D_MODEL] * jax.nn.sigmoid(glu[..., D_MODEL:])
        y_b = ssd_bidirectional(z, xbc, dt_raw, conv_w[layer], conv_b[layer], ssd_dt_bias[layer],
                                ssd_a_log[layer], ssd_d[layer], ssd_norm_w[layer], w_ssd_out[layer])
        merged = jax.nn.sigmoid(gate_a) * y_a + jax.nn.sigmoid(gate_b) * y_b
        h = h + merged @ w_o[layer]
        h = h + peer_ffn(rmsnorm(h, ffn_norm_w[layer]), peer_w_q[layer], peer_sub_keys[layer],
                         peer_u[layer], peer_v[layer])
    return rmsnorm(h, final_norm_w).astype(x.dtype)
```

```python
import functools

import jax
import jax.numpy as jnp
from jax import lax
from jax.experimental import pallas as pl
from jax.experimental.pallas import tpu as pltpu

F32 = jnp.float32
BF16 = jnp.bfloat16

RMS_EPS = 1e-6
SUBLANES = 8
LANES = 128
SSD_CHUNK = 128
SSD_HEADDIM = 64
SSD_GROUPS = 4
SSD_STATE = 64
PEER_TOPK = 16
VMEM_LIMIT = 56 * 1024 * 1024


def _params(sem):
    return pltpu.CompilerParams(dimension_semantics=sem, vmem_limit_bytes=VMEM_LIMIT)


def _rms(x, w):
    return x * lax.rsqrt(jnp.mean(x * x, axis=-1, keepdims=True) + RMS_EPS) * w


def _dot(a, b):
    return jnp.dot(a, b, preferred_element_type=F32)


def _full(shape):
    n = len(shape)
    return pl.BlockSpec(shape, lambda *_: (0,) * n)


def _inproj_body(x_ref, nw_ref, wu_ref, wz_ref, wx_ref, wdf_ref, wdb_ref, wga_ref, wgb_ref,
                 u_ref, z_ref, xbc_ref, dtf_ref, dtb_ref, ga_ref, gb_ref):
    xb = _rms(x_ref[...], nw_ref[...]).astype(BF16)
    u_ref[...] = _dot(xb, wu_ref[...]).astype(BF16)
    z_ref[...] = _dot(xb, wz_ref[...])
    xbc_ref[...] = _dot(xb, wx_ref[...])
    dtf_ref[...] = _dot(xb, wdf_ref[...])
    dtb_ref[...] = _dot(xb, wdb_ref[...])
    ga_ref[...] = _dot(xb, wga_ref[...])
    gb_ref[...] = _dot(xb, wgb_ref[...])


def _in_proj(x2, nw, ws, tm=512):
    t, d = x2.shape
    widths = [w.shape[1] for w in ws]
    dts = [BF16] + [F32] * 6
    return pl.pallas_call(
        _inproj_body,
        grid=(t // tm,),
        in_specs=[pl.BlockSpec((tm, d), lambda i: (i, 0)), _full((1, d))] + [_full(w.shape) for w in ws],
        out_specs=[pl.BlockSpec((tm, n), lambda i: (i, 0)) for n in widths],
        out_shape=[jax.ShapeDtypeStruct((t, n), dt) for n, dt in zip(widths, dts)],
        compiler_params=_params(("parallel",)),
        name="in_proj",
    )(x2, nw, *ws)


S5_LANE_GROUP = 4


def _s5_body(u_ref, bd_ref, cd_ref, tab_ref, y_ref, s_ref, car_ref, *, reverse, lb, ns):
    @pl.when(pl.program_id(1) == 0)
    def _():
        car_ref[...] = jnp.zeros_like(car_ref)

    s_ref[...] = _dot(u_ref[...], bd_ref[...])
    nt = lb // SUBLANES
    edge = 0 if reverse else SUBLANES - 1

    for c0 in range(0, ns // LANES, S5_LANE_GROUP):
        los = [(c0 + g) * LANES for g in range(S5_LANE_GROUP)]

        def tile_step(i, carry, los=los):
            r = (nt - 1 - i) if reverse else i
            rows = pl.ds(pl.multiple_of(r * SUBLANES, SUBLANES), SUBLANES)
            new = []
            for g, lo in enumerate(los):
                re = s_ref[rows, lo:lo + LANES]
                im = s_ref[rows, ns + lo:ns + lo + LANES]
                for kk, k in enumerate((1, 2, 4)):
                    pr = tab_ref[2 * kk, :, lo:lo + LANES]
                    pi = tab_ref[2 * kk + 1, :, lo:lo + LANES]
                    sh = (SUBLANES - k) if reverse else k
                    rr = pltpu.roll(re, sh, 0)
                    ri = pltpu.roll(im, sh, 0)
                    re, im = re + pr * rr - pi * ri, im + pr * ri + pi * rr
                pr = tab_ref[6, :, lo:lo + LANES]
                pi = tab_ref[7, :, lo:lo + LANES]
                cr, ci = carry[2 * g], carry[2 * g + 1]
                re, im = re + pr * cr - pi * ci, im + pr * ci + pi * cr
                s_ref[rows, lo:lo + LANES] = re
                s_ref[rows, ns + lo:ns + lo + LANES] = im
                new.append(jnp.broadcast_to(re[edge:edge + 1, :], (SUBLANES, LANES)))
                new.append(jnp.broadcast_to(im[edge:edge + 1, :], (SUBLANES, LANES)))
            return tuple(new)

        carry0 = []
        for lo in los:
            carry0.append(car_ref[0, :, lo:lo + LANES])
            carry0.append(car_ref[1, :, lo:lo + LANES])
        carry = lax.fori_loop(0, nt, tile_step, tuple(carry0))
        for g, lo in enumerate(los):
            car_ref[0, :, lo:lo + LANES] = carry[2 * g]
            car_ref[1, :, lo:lo + LANES] = carry[2 * g + 1]

    y_ref[...] = _dot(s_ref[...].astype(BF16), cd_ref[...])


def _s5_dir(u, bd, cd, tab, bsz, seqlen, reverse, lb=256):
    t, w = u.shape
    ns = bd.shape[1] // 2
    nb = seqlen // lb

    def blk(b, j):
        return (b * nb + ((nb - 1 - j) if reverse else j), 0)

    return pl.pallas_call(
        functools.partial(_s5_body, reverse=reverse, lb=lb, ns=ns),
        grid=(bsz, nb),
        in_specs=[pl.BlockSpec((lb, w), blk), _full(bd.shape), _full(cd.shape), _full(tab.shape)],
        out_specs=pl.BlockSpec((lb, w), blk),
        out_shape=jax.ShapeDtypeStruct((t, w), F32),
        scratch_shapes=[pltpu.VMEM((lb, 2 * ns), F32), pltpu.VMEM((2, SUBLANES, ns), F32)],
        compiler_params=_params(("parallel", "arbitrary")),
        name="s5_bwd" if reverse else "s5_fwd",
    )(u, bd, cd, tab)


def _s5_prep(a_re, a_im, log_dt, b_re, b_im, c_re, c_im, reverse):
    g, p = a_re.shape
    c = b_re.shape[-1]
    ar = a_re.astype(F32)
    ai = a_im.astype(F32)
    step = jnp.exp(log_dt.astype(F32))[:, None]
    mag = jnp.exp(ar * step)
    lam_r = mag * jnp.cos(ai * step)
    lam_i = mag * jnp.sin(ai * step)
    den = ar * ar + ai * ai
    f_r = ((lam_r - 1.0) * ar + lam_i * ai) / den
    f_i = (lam_i * ar - (lam_r - 1.0) * ai) / den
    bt_r = f_r[..., None] * b_re - f_i[..., None] * b_im
    bt_i = f_r[..., None] * b_im + f_i[..., None] * b_re
    eye = jnp.eye(g, dtype=F32)
    bd = jnp.concatenate([jnp.einsum('gpc,gh->gchp', bt_r, eye).reshape(g * c, g * p),
                          jnp.einsum('gpc,gh->gchp', bt_i, eye).reshape(g * c, g * p)], axis=1)
    cd = jnp.concatenate([jnp.einsum('gcp,gh->gphc', c_re, eye).reshape(g * p, g * c),
                          -jnp.einsum('gcp,gh->gphc', c_im, eye).reshape(g * p, g * c)], axis=0)
    lr = lam_r.reshape(1, g * p)
    li = lam_i.reshape(1, g * p)
    pw = [(lr, li)]
    for _ in range(SUBLANES - 1):
        qr, qi = pw[-1]
        pw.append((qr * lr - qi * li, qr * li + qi * lr))
    row = jnp.arange(SUBLANES)[:, None]
    tabs = []
    for k in (1, 2, 4):
        keep = (row < SUBLANES - k) if reverse else (row >= k)
        for comp in pw[k - 1]:
            tabs.append(jnp.where(keep, comp, 0.0))
    order = list(range(SUBLANES - 1, -1, -1)) if reverse else list(range(SUBLANES))
    for comp in (0, 1):
        tabs.append(jnp.concatenate([pw[k][comp] for k in order], axis=0))
    return bd.astype(BF16), cd.astype(BF16), jnp.stack(tabs, axis=0)


def _conv_body(cur_ref, prev_ref, nxt_ref, w_ref, b_ref, xs_ref, bc_ref, *, nblk, width, d_inner):
    j = pl.program_id(0) % nblk
    pad = width // 2
    cur = cur_ref[...]
    prev = jnp.where(j == 0, 0.0, prev_ref[...])
    nxt = jnp.where(j == nblk - 1, 0.0, nxt_ref[...])
    ext = jnp.concatenate([prev, cur, nxt], axis=0)
    tm = cur.shape[0]
    acc = jnp.zeros_like(cur) + b_ref[...]
    for k in range(width):
        off = SUBLANES - pad + k
        acc = acc + ext[off:off + tm, :] * w_ref[k:k + 1, :]
    act = acc * jax.nn.sigmoid(acc)
    xs_ref[...] = act[:, :d_inner]
    bc_ref[...] = act[:, d_inner:].astype(BF16)


def _conv(xbc, w, b, seqlen, d_inner, tm=256):
    t, cdim = xbc.shape
    width = w.shape[0]
    nblk = seqlen // tm
    r = tm // SUBLANES
    last = t // SUBLANES - 1
    return pl.pallas_call(
        functools.partial(_conv_body, nblk=nblk, width=width, d_inner=d_inner),
        grid=(t // tm,),
        in_specs=[pl.BlockSpec((tm, cdim), lambda i: (i, 0)),
                  pl.BlockSpec((SUBLANES, cdim), lambda i: (jnp.maximum(i * r - 1, 0), 0)),
                  pl.BlockSpec((SUBLANES, cdim), lambda i: (jnp.minimum((i + 1) * r, last), 0)),
                  _full(w.shape), _full(b.shape)],
        out_specs=[pl.BlockSpec((tm, d_inner), lambda i: (i, 0)),
                   pl.BlockSpec((tm, cdim - d_inner), lambda i: (i, 0))],
        out_shape=[jax.ShapeDtypeStruct((t, d_inner), F32),
                   jax.ShapeDtypeStruct((t, cdim - d_inner), BF16)],
        compiler_params=_params(("parallel",)),
        name="ssd_conv",
    )(xbc, xbc, xbc, w, b)


def _cumsum_rows(x):
    n = x.shape[0]
    ri = lax.broadcasted_iota(jnp.int32, x.shape, 0)
    k = 1
    while k < n:
        x = x + jnp.where(ri >= k, pltpu.roll(x, k, 0), 0.0)
        k *= 2
    return x


def _softplus(x):
    return jnp.maximum(x, 0.0) + jnp.log1p(jnp.exp(-jnp.abs(x)))


def _ssd_body(xs_ref, bc_ref, dt_ref, bias_ref, a_ref, y_ref, st_ref, *, reverse, heads):
    @pl.when(pl.program_id(1) == 0)
    def _():
        st_ref[...] = jnp.zeros_like(st_ref)

    n = SSD_CHUNK
    p = SSD_HEADDIM
    ng = SSD_GROUPS
    ns = SSD_STATE
    hg = heads // ng
    dt = _softplus(dt_ref[...] + bias_ref[...])
    adt = dt * a_ref[...]
    cs = _cumsum_rows(adt)
    tot = cs[n - 1:n, :]
    if reverse:
        cs = tot - cs + adt
    cs_t = cs.T
    dt_t = dt.T
    ecs = jnp.exp(cs)
    dst = jnp.exp(tot - cs) * dt
    etot = jnp.exp(tot)
    li = lax.broadcasted_iota(jnp.int32, (n, n), 0)
    si = lax.broadcasted_iota(jnp.int32, (n, n), 1)
    mask = (si >= li) if reverse else (si <= li)
    for g in range(ng):
        bg = bc_ref[:, g * ns:(g + 1) * ns]
        cg = bc_ref[:, ng * ns + g * ns:ng * ns + (g + 1) * ns]
        cb = lax.dot_general(cg, bg, (((1,), (1,)), ((), ())), preferred_element_type=F32)
        for hh in range(hg):
            h = g * hg + hh
            xh = xs_ref[:, h * p:(h + 1) * p]
            lm = jnp.exp(jnp.where(mask, cs[:, h:h + 1] - cs_t[h:h + 1, :], -jnp.inf))
            m = (cb * lm * dt_t[h:h + 1, :]).astype(BF16)
            yd = _dot(m, xh.astype(BF16))
            sp = st_ref[h]
            yo = _dot(cg, sp.astype(BF16)) * ecs[:, h:h + 1]
            y_ref[:, h * p:(h + 1) * p] = yd + yo
            xw = (xh * dst[:, h:h + 1]).astype(BF16)
            stn = lax.dot_general(bg, xw, (((0,), (0,)), ((), ())), preferred_element_type=F32)
            st_ref[h] = sp * etot[:, h:h + 1] + stn


def _ssd_dir(xs, bc, dt, bias, a, bsz, seqlen, heads, reverse):
    t, d_inner = xs.shape
    n = SSD_CHUNK
    nc = seqlen // n

    def blk(b, j):
        return (b * nc + ((nc - 1 - j) if reverse else j), 0)

    return pl.pallas_call(
        functools.partial(_ssd_body, reverse=reverse, heads=heads),
        grid=(bsz, nc),
        in_specs=[pl.BlockSpec((n, d_inner), blk), pl.BlockSpec((n, bc.shape[1]), blk),
                  pl.BlockSpec((n, LANES), blk), _full(bias.shape), _full(a.shape)],
        out_specs=pl.BlockSpec((n, d_inner), blk),
        out_shape=jax.ShapeDtypeStruct((t, d_inner), F32),
        scratch_shapes=[pltpu.VMEM((heads, SSD_STATE, SSD_HEADDIM), F32)],
        compiler_params=_params(("parallel", "arbitrary")),
        name="ssd_bwd" if reverse else "ssd_fwd",
    )(xs, bc, dt, bias, a)


def _gelu(x):
    return 0.5 * x * (1.0 + lax.erf(x * (2.0 ** -0.5)))


def _merge_body(x_ref, u_ref, y5f_ref, y5b_ref, d5_ref, wglu_ref, ga_ref,
                xs_ref, ysf_ref, ysb_ref, dss_ref, z_ref, nw_ref, wso_ref, gb_ref, wo_ref, fnw_ref,
                h_ref, xn_ref, *, d_model, groups):
    ya = u_ref[...].astype(F32) * d5_ref[...] + y5f_ref[...] + y5b_ref[...]
    glu = _dot(_gelu(ya).astype(BF16), wglu_ref[...])
    ya = glu[:, :d_model] * jax.nn.sigmoid(glu[:, d_model:])
    z = z_ref[...]
    yb = (xs_ref[...] * dss_ref[...] + ysf_ref[...] + ysb_ref[...]) * (z * jax.nn.sigmoid(z))
    gw = yb.shape[1] // groups
    parts = []
    for g in range(groups):
        yg = yb[:, g * gw:(g + 1) * gw]
        parts.append(yg * lax.rsqrt(jnp.mean(yg * yg, axis=-1, keepdims=True) + RMS_EPS))
    yb = jnp.concatenate(parts, axis=1) * nw_ref[...]
    yb = _dot(yb.astype(BF16), wso_ref[...])
    merged = jax.nn.sigmoid(ga_ref[...]) * ya + jax.nn.sigmoid(gb_ref[...]) * yb
    h = x_ref[...] + _dot(merged.astype(BF16), wo_ref[...])
    h_ref[...] = h
    xn_ref[...] = _rms(h, fnw_ref[...])


def _merge(x2, u, y5f, y5b, d5, wglu, ga, xs, ysf, ysb, dss, z, nw, wso, gb, wo, fnw, tm=256):
    t, d = x2.shape
    args = [x2, u, y5f, y5b, d5, wglu, ga, xs, ysf, ysb, dss, z, nw, wso, gb, wo, fnw]
    per_token = [True, True, True, True, False, False, True, True, True, True, False, True, False, False, True,
                 False, False]
    specs = []
    for a, tok in zip(args, per_token):
        if tok:
            specs.append(pl.BlockSpec((tm, a.shape[1]), lambda i: (i, 0)))
        else:
            specs.append(_full(a.shape))
    return pl.pallas_call(
        functools.partial(_merge_body, d_model=d, groups=SSD_GROUPS),
        grid=(t // tm,),
        in_specs=specs,
        out_specs=[pl.BlockSpec((tm, d), lambda i: (i, 0))] * 2,
        out_shape=[jax.ShapeDtypeStruct((t, d), F32)] * 2,
        compiler_params=_params(("parallel",)),
        name="merge",
    )(*args)


def _mixer(x2, bsz, seqlen, mix_norm_w, w_in, s5_a_re, s5_a_im, s5_log_dt, s5_b_re, s5_b_im, s5_c_re, s5_c_im,
           s5_d, w_glu, conv_w, conv_b, ssd_dt_bias, ssd_a_log, ssd_d, ssd_norm_w, w_ssd_out, w_o, ffn_norm_w):
    t, d = x2.shape
    s5w = s5_d.shape[0]
    d_inner = w_ssd_out.shape[0]
    heads = ssd_d.shape[0]
    cdim = conv_b.shape[0]
    o1 = s5w
    o2 = o1 + d_inner
    o3 = o2 + cdim
    o4 = o3 + 2 * heads

    def padw(w):
        return jnp.pad(w, ((0, 0), (0, LANES - w.shape[1])))

    ws = [w_in[:, :o1], w_in[:, o1:o2], w_in[:, o2:o3], padw(w_in[:, o3:o3 + heads]),
          padw(w_in[:, o3 + heads:o4]), w_in[:, o4:o4 + d], w_in[:, o4 + d:]]
    ws = [w.astype(BF16) for w in ws]
    u, z, xbc, dtf, dtb, ga, gb = _in_proj(x2, mix_norm_w.reshape(1, d), ws)

    y5 = []
    for rev in (False, True):
        k = int(rev)
        bd, cd, tab = _s5_prep(s5_a_re[k], s5_a_im[k], s5_log_dt[k], s5_b_re[k], s5_b_im[k],
                               s5_c_re[k], s5_c_im[k], rev)
        y5.append(_s5_dir(u, bd, cd, tab, bsz, seqlen, rev))

    xs, bc = _conv(xbc, conv_w.reshape(conv_w.shape[0], cdim), conv_b.reshape(1, cdim), seqlen, d_inner)
    ys = []
    for rev in (False, True):
        k = int(rev)
        bias = jnp.pad(ssd_dt_bias[k].astype(F32), (0, LANES - heads)).reshape(1, LANES)
        a = jnp.pad(-jnp.exp(ssd_a_log[k].astype(F32)), (0, LANES - heads)).reshape(1, LANES)
        ys.append(_ssd_dir(xs, bc, dtb if rev else dtf, bias, a, bsz, seqlen, heads, rev))

    dss = jnp.repeat(ssd_d.astype(F32), d_inner // heads).reshape(1, d_inner)
    return _merge(x2, u, y5[0], y5[1], s5_d.reshape(1, s5w).astype(F32), w_glu.astype(BF16), ga,
                  xs, ys[0], ys[1], dss, z, ssd_norm_w.reshape(1, d_inner).astype(F32),
                  w_ssd_out.astype(BF16), gb, w_o.astype(BF16), ffn_norm_w.reshape(1, d).astype(F32))


def _split_bf16(x):
    hi = x.astype(BF16)
    return hi, (x - hi.astype(F32)).astype(BF16)


def _dot3(ah, al, bh, bl, dims=(((1,), (0,)), ((), ()))):
    def dg(p, q):
        return lax.dot_general(p, q, dims, preferred_element_type=F32)
    return dg(ah, bh) + (dg(al, bh) + dg(ah, bl))


def _topk_rows(x, k):
    n = x.shape[0]
    ri = lax.broadcasted_iota(jnp.int32, x.shape, 0)
    vals, poss = [], []
    for _ in range(k):
        m = jnp.max(x, axis=0, keepdims=True)
        pos = jnp.min(jnp.where(x == m, ri, n), axis=0, keepdims=True)
        vals.append(m)
        poss.append(pos)
        x = jnp.where(ri == pos, -jnp.inf, x)
    return jnp.concatenate(vals, axis=0), jnp.concatenate(poss, axis=0)


def _pick_rows(table, sel):
    out = jnp.zeros(sel.shape, table.dtype)
    for a in range(table.shape[0]):
        out = jnp.where(sel == a, table[a:a + 1, :], out)
    return out


def _route_body(xn_ref, wqh_ref, wql_ref, kh_ref, kl_ref, e_ref, g_ref, q_ref, *, heads, nk, dk):
    k = PEER_TOPK
    xh, xl = _split_bf16(xn_ref[...])
    q_ref[...] = _dot3(xh, xl, wqh_ref[...], wql_ref[...])
    nt = (((1,), (1,)), ((), ()))

    def head(h, _):
        tops = []
        for s in range(2):
            col = pl.multiple_of((h * 2 + s) * dk, dk)
            qh, ql = _split_bf16(q_ref[:, pl.ds(col, dk)])
            sc = _dot3(kh_ref[h * 2 + s], kl_ref[h * 2 + s], qh, ql, nt)
            tops.append(_topk_rows(sc, k))
        (s1, i1), (s2, i2) = tops
        cand = jnp.concatenate([s1[a:a + 1, :] + s2 for a in range(k)], axis=0)
        best, pos = _topk_rows(cand, k)
        expert = _pick_rows(i1, pos // k) * nk + _pick_rows(i2, pos % k)
        w = jnp.exp(best - best[0:1, :])
        gate = w / jnp.sum(w, axis=0, keepdims=True)
        rows = pl.ds(pl.multiple_of(h * k, k), k)
        e_ref[0, rows, :] = expert
        g_ref[0, rows, :] = gate
        return 0

    lax.fori_loop(0, heads, head, 0)


def _peer_route(xn, wqh, wql, kh, kl, heads, nk, tm=LANES):
    t, d = xn.shape
    dk = kh.shape[-1]
    nb = t // tm
    slots = heads * PEER_TOPK
    return pl.pallas_call(
        functools.partial(_route_body, heads=heads, nk=nk, dk=dk),
        grid=(nb,),
        in_specs=[pl.BlockSpec((tm, d), lambda i: (i, 0)), _full(wqh.shape), _full(wql.shape),
                  _full(kh.shape), _full(kl.shape)],
        out_specs=[pl.BlockSpec((1, slots, tm), lambda i: (i, 0, 0))] * 2,
        out_shape=[jax.ShapeDtypeStruct((nb, slots, tm), jnp.int32),
                   jax.ShapeDtypeStruct((nb, slots, tm), F32)],
        scratch_shapes=[pltpu.VMEM((tm, wqh.shape[1]), F32)],
        compiler_params=_params(("parallel",)),
        name="peer_route",
    )(xn, wqh, wql, kh, kl)


SC_CORES = 2
SC_SUBCORES = 16
GATHER_WINDOW = 32


def _gather_rows(table, idx):
    from jax.experimental.pallas import tpu_sc as plsc
    n = idx.shape[0]
    dm = table.shape[1]
    workers = SC_CORES * SC_SUBCORES
    per = n // workers
    nwin = per // GATHER_WINDOW
    idx3 = idx.reshape(workers, nwin, GATHER_WINDOW)
    mesh = plsc.VectorSubcoreMesh(core_axis_name="c", subcore_axis_name="s")

    @functools.partial(
        pl.kernel, mesh=mesh,
        out_type=jax.ShapeDtypeStruct((n, dm), table.dtype),
        scratch_types=[pltpu.VMEM((nwin, GATHER_WINDOW), jnp.int32),
                       pltpu.VMEM((GATHER_WINDOW, dm), table.dtype),
                       pltpu.SemaphoreType.DMA],
        name="peer_gather",
    )
    def gather(table_hbm, idx_hbm, out_hbm, idx_v, rows_v, sem):
        wid = lax.axis_index("s") * SC_CORES + lax.axis_index("c")
        pltpu.sync_copy(idx_hbm.at[wid], idx_v)

        @pl.loop(0, nwin)
        def _(w):
            pltpu.async_copy(table_hbm.at[idx_v.at[w]], rows_v, sem).wait()
            pltpu.sync_copy(rows_v, out_hbm.at[pl.ds(wid * per + w * GATHER_WINDOW, GATHER_WINDOW)])

    return gather(table, idx3)


def _eval_body(ug_ref, vg_ref, xn_ref, g_ref, h_ref, fw_ref, o_ref, *, slots):
    r = SUBLANES
    x = xn_ref[...]
    lane = lax.broadcasted_iota(jnp.int32, (r, slots), 1)
    hid = jnp.zeros((r, slots), F32)
    for k in range(slots):
        hk = jnp.sum(ug_ref[k * r:(k + 1) * r, :] * x, axis=-1, keepdims=True)
        hid = jnp.where(lane == k, hk, hid)
    act = _gelu(hid) * g_ref[...]
    acc = jnp.zeros(x.shape, F32)
    for k in range(slots):
        acc = acc + act[:, k:k + 1] * vg_ref[k * r:(k + 1) * r, :]
    o_ref[...] = _rms(h_ref[...] + acc, fw_ref[...])


def _peer_eval(ug, vg, xn, gates, h, fw, slots):
    t, d = xn.shape
    r = SUBLANES
    return pl.pallas_call(
        functools.partial(_eval_body, slots=slots),
        grid=(t // r,),
        in_specs=[pl.BlockSpec((slots * r, d), lambda i: (i, 0)), pl.BlockSpec((slots * r, d), lambda i: (i, 0)),
                  pl.BlockSpec((r, d), lambda i: (i, 0)), pl.BlockSpec((r, slots), lambda i: (i, 0)),
                  pl.BlockSpec((r, d), lambda i: (i, 0)), _full(fw.shape)],
        out_specs=pl.BlockSpec((r, d), lambda i: (i, 0)),
        out_shape=jax.ShapeDtypeStruct((t, d), F32),
        compiler_params=_params(("parallel",)),
        name="peer_eval",
    )(ug, vg, xn, gates, h, fw)


PEER_CHUNK_TOKENS = 2048


def _peer(h, xn, w_q, sub_keys, u_tab, v_tab, final_norm_w):
    t, d = h.shape
    heads, _, nk, dk = sub_keys.shape
    slots = heads * PEER_TOPK
    wqh, wql = _split_bf16(w_q.astype(F32))
    kh, kl = _split_bf16(sub_keys.reshape(heads * 2, nk, dk).astype(F32))
    expert_t, gates_t = _peer_route(xn, wqh, wql, kh, kl, heads, nk)
    nb, _, tm = expert_t.shape
    r = SUBLANES
    idx = expert_t.reshape(nb, slots, tm // r, r).transpose(0, 2, 1, 3).reshape(t // r, slots * r)
    gates = gates_t.transpose(0, 2, 1).reshape(t, slots)
    fw = final_norm_w.reshape(1, d).astype(F32)
    tc = min(PEER_CHUNK_TOKENS, t)
    outs = []
    for c in range(t // tc):
        rows = slice(c * tc, (c + 1) * tc)
        ci = idx[c * tc // r:(c + 1) * tc // r].reshape(-1)
        ug = _gather_rows(u_tab, ci)
        vg = _gather_rows(v_tab, ci)
        outs.append(_peer_eval(ug, vg, xn[rows], gates[rows], h[rows], fw, slots))
    return jnp.concatenate(outs, axis=0)


def kernel(x, mix_norm_w, w_in, s5_a_re, s5_a_im, s5_log_dt, s5_b_re, s5_b_im, s5_c_re, s5_c_im, s5_d, w_glu, conv_w, conv_b, ssd_dt_bias, ssd_a_log, ssd_d, ssd_norm_w, w_ssd_out, w_o, ffn_norm_w, peer_w_q, peer_sub_keys, peer_u, peer_v, final_norm_w):
    bsz, seqlen, d = x.shape
    x2 = x.reshape(bsz * seqlen, d)
    h, xn = _mixer(x2, bsz, seqlen, mix_norm_w[0], w_in[0], s5_a_re[0], s5_a_im[0], s5_log_dt[0],
                   s5_b_re[0], s5_b_im[0], s5_c_re[0], s5_c_im[0], s5_d[0], w_glu[0], conv_w[0], conv_b[0],
                   ssd_dt_bias[0], ssd_a_log[0], ssd_d[0], ssd_norm_w[0], w_ssd_out[0], w_o[0], ffn_norm_w[0])
    out = _peer(h, xn, peer_w_q[0], peer_sub_keys[0], peer_u[0].astype(F32), peer_v[0].astype(F32), final_norm_w)
    return out.reshape(bsz, seqlen, d).astype(x.dtype)
```

```python
import functools

import jax
import jax.numpy as jnp
from jax import lax
from jax.experimental import pallas as pl
from jax.experimental.pallas import tpu as pltpu

F32 = jnp.float32
BF16 = jnp.bfloat16

RMS_EPS = 1e-6
SUBLANES = 8
LANES = 128
SSD_CHUNK = 128
SSD_HEADDIM = 64
SSD_GROUPS = 4
SSD_STATE = 64
PEER_TOPK = 16
VMEM_LIMIT = 56 * 1024 * 1024


def _params(sem):
    return pltpu.CompilerParams(dimension_semantics=sem, vmem_limit_bytes=VMEM_LIMIT)


def _rms(x, w):
    return x * lax.rsqrt(jnp.mean(x * x, axis=-1, keepdims=True) + RMS_EPS) * w


def _dot(a, b):
    return jnp.dot(a, b, preferred_element_type=F32)


def _full(shape):
    n = len(shape)
    return pl.BlockSpec(shape, lambda *_: (0,) * n)


def _inproj_body(x_ref, nw_ref, wu_ref, wz_ref, wx_ref, wdf_ref, wdb_ref, wga_ref, wgb_ref,
                 u_ref, z_ref, xbc_ref, dtf_ref, dtb_ref, ga_ref, gb_ref):
    xb = _rms(x_ref[...], nw_ref[...]).astype(BF16)
    u_ref[...] = _dot(xb, wu_ref[...]).astype(BF16)
    z_ref[...] = _dot(xb, wz_ref[...])
    xbc_ref[...] = _dot(xb, wx_ref[...])
    dtf_ref[...] = _dot(xb, wdf_ref[...])
    dtb_ref[...] = _dot(xb, wdb_ref[...])
    ga_ref[...] = _dot(xb, wga_ref[...])
    gb_ref[...] = _dot(xb, wgb_ref[...])


def _in_proj(x2, nw, ws, tm=512):
    t, d = x2.shape
    widths = [w.shape[1] for w in ws]
    dts = [BF16] + [F32] * 6
    return pl.pallas_call(
        _inproj_body,
        grid=(t // tm,),
        in_specs=[pl.BlockSpec((tm, d), lambda i: (i, 0)), _full((1, d))] + [_full(w.shape) for w in ws],
        out_specs=[pl.BlockSpec((tm, n), lambda i: (i, 0)) for n in widths],
        out_shape=[jax.ShapeDtypeStruct((t, n), dt) for n, dt in zip(widths, dts)],
        compiler_params=_params(("parallel",)),
        name="in_proj",
    )(x2, nw, *ws)


S5_LANE_GROUP = 4


def _s5_body(u_ref, bd_ref, cd_ref, tab_ref, y_ref, s_ref, car_ref, *, reverse, lb, ns):
    @pl.when(pl.program_id(1) == 0)
    def _():
        car_ref[...] = jnp.zeros_like(car_ref)

    s_ref[...] = _dot(u_ref[...], bd_ref[...])
    nt = lb // SUBLANES
    edge = 0 if reverse else SUBLANES - 1

    for c0 in range(0, ns // LANES, S5_LANE_GROUP):
        los = [(c0 + g) * LANES for g in range(S5_LANE_GROUP)]

        def tile_step(i, carry, los=los):
            r = (nt - 1 - i) if reverse else i
            rows = pl.ds(pl.multiple_of(r * SUBLANES, SUBLANES), SUBLANES)
            new = []
            for g, lo in enumerate(los):
                re = s_ref[rows, lo:lo + LANES]
                im = s_ref[rows, ns + lo:ns + lo + LANES]
                for kk, k in enumerate((1, 2, 4)):
                    pr = tab_ref[2 * kk, :, lo:lo + LANES]
                    pi = tab_ref[2 * kk + 1, :, lo:lo + LANES]
                    sh = (SUBLANES - k) if reverse else k
                    rr = pltpu.roll(re, sh, 0)
                    ri = pltpu.roll(im, sh, 0)
                    re, im = re + pr * rr - pi * ri, im + pr * ri + pi * rr
                pr = tab_ref[6, :, lo:lo + LANES]
                pi = tab_ref[7, :, lo:lo + LANES]
                cr, ci = carry[2 * g], carry[2 * g + 1]
                re, im = re + pr * cr - pi * ci, im + pr * ci + pi * cr
                s_ref[rows, lo:lo + LANES] = re
                s_ref[rows, ns + lo:ns + lo + LANES] = im
                new.append(jnp.broadcast_to(re[edge:edge + 1, :], (SUBLANES, LANES)))
                new.append(jnp.broadcast_to(im[edge:edge + 1, :], (SUBLANES, LANES)))
            return tuple(new)

        carry0 = []
        for lo in los:
            carry0.append(car_ref[0, :, lo:lo + LANES])
            carry0.append(car_ref[1, :, lo:lo + LANES])
        carry = lax.fori_loop(0, nt, tile_step, tuple(carry0))
        for g, lo in enumerate(los):
            car_ref[0, :, lo:lo + LANES] = carry[2 * g]
            car_ref[1, :, lo:lo + LANES] = carry[2 * g + 1]

    y_ref[...] = _dot(s_ref[...].astype(BF16), cd_ref[...])


def _s5_dir(u, bd, cd, tab, bsz, seqlen, reverse, lb=256):
    t, w = u.shape
    ns = bd.shape[1] // 2
    nb = seqlen // lb

    def blk(b, j):
        return (b * nb + ((nb - 1 - j) if reverse else j), 0)

    return pl.pallas_call(
        functools.partial(_s5_body, reverse=reverse, lb=lb, ns=ns),
        grid=(bsz, nb),
        in_specs=[pl.BlockSpec((lb, w), blk), _full(bd.shape), _full(cd.shape), _full(tab.shape)],
        out_specs=pl.BlockSpec((lb, w), blk),
        out_shape=jax.ShapeDtypeStruct((t, w), F32),
        scratch_shapes=[pltpu.VMEM((lb, 2 * ns), F32), pltpu.VMEM((2, SUBLANES, ns), F32)],
        compiler_params=_params(("parallel", "arbitrary")),
        name="s5_bwd" if reverse else "s5_fwd",
    )(u, bd, cd, tab)


def _s5_prep(a_re, a_im, log_dt, b_re, b_im, c_re, c_im, reverse):
    g, p = a_re.shape
    c = b_re.shape[-1]
    ar = a_re.astype(F32)
    ai = a_im.astype(F32)
    step = jnp.exp(log_dt.astype(F32))[:, None]
    mag = jnp.exp(ar * step)
    lam_r = mag * jnp.cos(ai * step)
    lam_i = mag * jnp.sin(ai * step)
    den = ar * ar + ai * ai
    f_r = ((lam_r - 1.0) * ar + lam_i * ai) / den
    f_i = (lam_i * ar - (lam_r - 1.0) * ai) / den
    bt_r = f_r[..., None] * b_re - f_i[..., None] * b_im
    bt_i = f_r[..., None] * b_im + f_i[..., None] * b_re
    eye = jnp.eye(g, dtype=F32)
    bd = jnp.concatenate([jnp.einsum('gpc,gh->gchp', bt_r, eye).reshape(g * c, g * p),
                          jnp.einsum('gpc,gh->gchp', bt_i, eye).reshape(g * c, g * p)], axis=1)
    cd = jnp.concatenate([jnp.einsum('gcp,gh->gphc', c_re, eye).reshape(g * p, g * c),
                          -jnp.einsum('gcp,gh->gphc', c_im, eye).reshape(g * p, g * c)], axis=0)
    lr = lam_r.reshape(1, g * p)
    li = lam_i.reshape(1, g * p)
    pw = [(lr, li)]
    for _ in range(SUBLANES - 1):
        qr, qi = pw[-1]
        pw.append((qr * lr - qi * li, qr * li + qi * lr))
    row = jnp.arange(SUBLANES)[:, None]
    tabs = []
    for k in (1, 2, 4):
        keep = (row < SUBLANES - k) if reverse else (row >= k)
        for comp in pw[k - 1]:
            tabs.append(jnp.where(keep, comp, 0.0))
    order = list(range(SUBLANES - 1, -1, -1)) if reverse else list(range(SUBLANES))
    for comp in (0, 1):
        tabs.append(jnp.concatenate([pw[k][comp] for k in order], axis=0))
    return bd.astype(BF16), cd.astype(BF16), jnp.stack(tabs, axis=0)


def _conv_body(cur_ref, prev_ref, nxt_ref, w_ref, b_ref, xs_ref, bc_ref, *, nblk, width, d_inner):
    j = pl.program_id(0) % nblk
    pad = width // 2
    cur = cur_ref[...]
    prev = jnp.where(j == 0, 0.0, prev_ref[...])
    nxt = jnp.where(j == nblk - 1, 0.0, nxt_ref[...])
    ext = jnp.concatenate([prev, cur, nxt], axis=0)
    tm = cur.shape[0]
    acc = jnp.zeros_like(cur) + b_ref[...]
    for k in range(width):
        off = SUBLANES - pad + k
        acc = acc + ext[off:off + tm, :] * w_ref[k:k + 1, :]
    act = acc * jax.nn.sigmoid(acc)
    xs_ref[...] = act[:, :d_inner]
    bc_ref[...] = act[:, d_inner:].astype(BF16)


def _conv(xbc, w, b, seqlen, d_inner, tm=256):
    t, cdim = xbc.shape
    width = w.shape[0]
    nblk = seqlen // tm
    r = tm // SUBLANES
    last = t // SUBLANES - 1
    return pl.pallas_call(
        functools.partial(_conv_body, nblk=nblk, width=width, d_inner=d_inner),
        grid=(t // tm,),
        in_specs=[pl.BlockSpec((tm, cdim), lambda i: (i, 0)),
                  pl.BlockSpec((SUBLANES, cdim), lambda i: (jnp.maximum(i * r - 1, 0), 0)),
                  pl.BlockSpec((SUBLANES, cdim), lambda i: (jnp.minimum((i + 1) * r, last), 0)),
                  _full(w.shape), _full(b.shape)],
        out_specs=[pl.BlockSpec((tm, d_inner), lambda i: (i, 0)),
                   pl.BlockSpec((tm, cdim - d_inner), lambda i: (i, 0))],
        out_shape=[jax.ShapeDtypeStruct((t, d_inner), F32),
                   jax.ShapeDtypeStruct((t, cdim - d_inner), BF16)],
        compiler_params=_params(("parallel",)),
        name="ssd_conv",
    )(xbc, xbc, xbc, w, b)


def _cumsum_rows(x):
    n = x.shape[0]
    ri = lax.broadcasted_iota(jnp.int32, x.shape, 0)
    k = 1
    while k < n:
        x = x + jnp.where(ri >= k, pltpu.roll(x, k, 0), 0.0)
        k *= 2
    return x


def _softplus(x):
    return jnp.maximum(x, 0.0) + jnp.log1p(jnp.exp(-jnp.abs(x)))


def _ssd_body(xs_ref, bc_ref, dt_ref, bias_ref, a_ref, y_ref, st_ref, *, reverse, heads):
    @pl.when(pl.program_id(1) == 0)
    def _():
        st_ref[...] = jnp.zeros_like(st_ref)

    n = SSD_CHUNK
    p = SSD_HEADDIM
    ng = SSD_GROUPS
    ns = SSD_STATE
    hg = heads // ng
    dt = _softplus(dt_ref[...] + bias_ref[...])
    adt = dt * a_ref[...]
    cs = _cumsum_rows(adt)
    tot = cs[n - 1:n, :]
    if reverse:
        cs = tot - cs + adt
    cs_t = cs.T
    dt_t = dt.T
    ecs = jnp.exp(cs)
    dst = jnp.exp(tot - cs) * dt
    etot = jnp.exp(tot)
    li = lax.broadcasted_iota(jnp.int32, (n, n), 0)
    si = lax.broadcasted_iota(jnp.int32, (n, n), 1)
    mask = (si >= li) if reverse else (si <= li)
    for g in range(ng):
        bg = bc_ref[:, g * ns:(g + 1) * ns]
        cg = bc_ref[:, ng * ns + g * ns:ng * ns + (g + 1) * ns]
        cb = lax.dot_general(cg, bg, (((1,), (1,)), ((), ())), preferred_element_type=F32)
        for hh in range(hg):
            h = g * hg + hh
            xh = xs_ref[:, h * p:(h + 1) * p]
            lm = jnp.exp(jnp.where(mask, cs[:, h:h + 1] - cs_t[h:h + 1, :], -jnp.inf))
            m = (cb * lm * dt_t[h:h + 1, :]).astype(BF16)
            yd = _dot(m, xh.astype(BF16))
            sp = st_ref[h]
            yo = _dot(cg, sp.astype(BF16)) * ecs[:, h:h + 1]
            y_ref[:, h * p:(h + 1) * p] = yd + yo
            xw = (xh * dst[:, h:h + 1]).astype(BF16)
            stn = lax.dot_general(bg, xw, (((0,), (0,)), ((), ())), preferred_element_type=F32)
            st_ref[h] = sp * etot[:, h:h + 1] + stn


def _ssd_dir(xs, bc, dt, bias, a, bsz, seqlen, heads, reverse):
    t, d_inner = xs.shape
    n = SSD_CHUNK
    nc = seqlen // n

    def blk(b, j):
        return (b * nc + ((nc - 1 - j) if reverse else j), 0)

    return pl.pallas_call(
        functools.partial(_ssd_body, reverse=reverse, heads=heads),
        grid=(bsz, nc),
        in_specs=[pl.BlockSpec((n, d_inner), blk), pl.BlockSpec((n, bc.shape[1]), blk),
                  pl.BlockSpec((n, LANES), blk), _full(bias.shape), _full(a.shape)],
        out_specs=pl.BlockSpec((n, d_inner), blk),
        out_shape=jax.ShapeDtypeStruct((t, d_inner), F32),
        scratch_shapes=[pltpu.VMEM((heads, SSD_STATE, SSD_HEADDIM), F32)],
        compiler_params=_params(("parallel", "arbitrary")),
        name="ssd_bwd" if reverse else "ssd_fwd",
    )(xs, bc, dt, bias, a)


def _gelu(x):
    return 0.5 * x * (1.0 + lax.erf(x * (2.0 ** -0.5)))


def _merge_body(x_ref, u_ref, y5f_ref, y5b_ref, d5_ref, wglu_ref, ga_ref,
                xs_ref, ysf_ref, ysb_ref, dss_ref, z_ref, nw_ref, wso_ref, gb_ref, wo_ref, fnw_ref,
                h_ref, xn_ref, *, d_model, groups):
    ya = u_ref[...].astype(F32) * d5_ref[...] + y5f_ref[...] + y5b_ref[...]
    glu = _dot(_gelu(ya).astype(BF16), wglu_ref[...])
    ya = glu[:, :d_model] * jax.nn.sigmoid(glu[:, d_model:])
    z = z_ref[...]
    yb = (xs_ref[...] * dss_ref[...] + ysf_ref[...] + ysb_ref[...]) * (z * jax.nn.sigmoid(z))
    gw = yb.shape[1] // groups
    parts = []
    for g in range(groups):
        yg = yb[:, g * gw:(g + 1) * gw]
        parts.append(yg * lax.rsqrt(jnp.mean(yg * yg, axis=-1, keepdims=True) + RMS_EPS))
    yb = jnp.concatenate(parts, axis=1) * nw_ref[...]
    yb = _dot(yb.astype(BF16), wso_ref[...])
    merged = jax.nn.sigmoid(ga_ref[...]) * ya + jax.nn.sigmoid(gb_ref[...]) * yb
    h = x_ref[...] + _dot(merged.astype(BF16), wo_ref[...])
    h_ref[...] = h
    xn_ref[...] = _rms(h, fnw_ref[...])


def _merge(x2, u, y5f, y5b, d5, wglu, ga, xs, ysf, ysb, dss, z, nw, wso, gb, wo, fnw, tm=256):
    t, d = x2.shape
    args = [x2, u, y5f, y5b, d5, wglu, ga, xs, ysf, ysb, dss, z, nw, wso, gb, wo, fnw]
    per_token = [True, True, True, True, False, False, True, True, True, True, False, True, False, False, True,
                 False, False]
    specs = []
    for a, tok in zip(args, per_token):
        if tok:
            specs.append(pl.BlockSpec((tm, a.shape[1]), lambda i: (i, 0)))
        else:
            specs.append(_full(a.shape))
    return pl.pallas_call(
        functools.partial(_merge_body, d_model=d, groups=SSD_GROUPS),
        grid=(t // tm,),
        in_specs=specs,
        out_specs=[pl.BlockSpec((tm, d), lambda i: (i, 0))] * 2,
        out_shape=[jax.ShapeDtypeStruct((t, d), F32)] * 2,
        compiler_params=_params(("parallel",)),
        name="merge",
    )(*args)


def _mixer(x2, bsz, seqlen, mix_norm_w, w_in, s5_a_re, s5_a_im, s5_log_dt, s5_b_re, s5_b_im, s5_c_re, s5_c_im,
           s5_d, w_glu, conv_w, conv_b, ssd_dt_bias, ssd_a_log, ssd_d, ssd_norm_w, w_ssd_out, w_o, ffn_norm_w):
    t, d = x2.shape
    s5w = s5_d.shape[0]
    d_inner = w_ssd_out.shape[0]
    heads = ssd_d.shape[0]
    cdim = conv_b.shape[0]
    o1 = s5w
    o2 = o1 + d_inner
    o3 = o2 + cdim
    o4 = o3 + 2 * heads

    def padw(w):
        return jnp.pad(w, ((0, 0), (0, LANES - w.shape[1])))

    ws = [w_in[:, :o1], w_in[:, o1:o2], w_in[:, o2:o3], padw(w_in[:, o3:o3 + heads]),
          padw(w_in[:, o3 + heads:o4]), w_in[:, o4:o4 + d], w_in[:, o4 + d:]]
    ws = [w.astype(BF16) for w in ws]
    u, z, xbc, dtf, dtb, ga, gb = _in_proj(x2, mix_norm_w.reshape(1, d), ws)

    y5 = []
    for rev in (False, True):
        k = int(rev)
        bd, cd, tab = _s5_prep(s5_a_re[k], s5_a_im[k], s5_log_dt[k], s5_b_re[k], s5_b_im[k],
                               s5_c_re[k], s5_c_im[k], rev)
        y5.append(_s5_dir(u, bd, cd, tab, bsz, seqlen, rev))

    xs, bc = _conv(xbc, conv_w.reshape(conv_w.shape[0], cdim), conv_b.reshape(1, cdim), seqlen, d_inner)
    ys = []
    for rev in (False, True):
        k = int(rev)
        bias = jnp.pad(ssd_dt_bias[k].astype(F32), (0, LANES - heads)).reshape(1, LANES)
        a = jnp.pad(-jnp.exp(ssd_a_log[k].astype(F32)), (0, LANES - heads)).reshape(1, LANES)
        ys.append(_ssd_dir(xs, bc, dtb if rev else dtf, bias, a, bsz, seqlen, heads, rev))

    dss = jnp.repeat(ssd_d.astype(F32), d_inner // heads).reshape(1, d_inner)
    return _merge(x2, u, y5[0], y5[1], s5_d.reshape(1, s5w).astype(F32), w_glu.astype(BF16), ga,
                  xs, ys[0], ys[1], dss, z, ssd_norm_w.reshape(1, d_inner).astype(F32),
                  w_ssd_out.astype(BF16), gb, w_o.astype(BF16), ffn_norm_w.reshape(1, d).astype(F32))


def _split_bf16(x):
    hi = x.astype(BF16)
    return hi, (x - hi.astype(F32)).astype(BF16)


def _dot3(ah, al, bh, bl, dims=(((1,), (0,)), ((), ()))):
    def dg(p, q):
        return lax.dot_general(p, q, dims, preferred_element_type=F32)
    return dg(ah, bh) + (dg(al, bh) + dg(ah, bl))


def _topk_rows(x, k):
    n = x.shape[0]
    ri = lax.broadcasted_iota(jnp.int32, x.shape, 0)
    vals, poss = [], []
    for _ in range(k):
        m = jnp.max(x, axis=0, keepdims=True)
        pos = jnp.min(jnp.where(x == m, ri, n), axis=0, keepdims=True)
        vals.append(m)
        poss.append(pos)
        x = jnp.where(ri == pos, -jnp.inf, x)
    return jnp.concatenate(vals, axis=0), jnp.concatenate(poss, axis=0)


def _pick_rows(table, sel):
    out = jnp.zeros(sel.shape, table.dtype)
    for a in range(table.shape[0]):
        out = jnp.where(sel == a, table[a:a + 1, :], out)
    return out


def _route_body(xn_ref, wqh_ref, wql_ref, kh_ref, kl_ref, e_ref, g_ref, q_ref, *, heads, nk, dk):
    k = PEER_TOPK
    xh, xl = _split_bf16(xn_ref[...])
    q_ref[...] = _dot3(xh, xl, wqh_ref[...], wql_ref[...])
    nt = (((1,), (1,)), ((), ()))

    def head(h, _):
        tops = []
        for s in range(2):
            col = pl.multiple_of((h * 2 + s) * dk, dk)
            qh, ql = _split_bf16(q_ref[:, pl.ds(col, dk)])
            sc = _dot3(kh_ref[h * 2 + s], kl_ref[h * 2 + s], qh, ql, nt)
            tops.append(_topk_rows(sc, k))
        (s1, i1), (s2, i2) = tops
        cand = jnp.concatenate([s1[a:a + 1, :] + s2 for a in range(k)], axis=0)
        best, pos = _topk_rows(cand, k)
        expert = _pick_rows(i1, pos // k) * nk + _pick_rows(i2, pos % k)
        w = jnp.exp(best - best[0:1, :])
        gate = w / jnp.sum(w, axis=0, keepdims=True)
        rows = pl.ds(pl.multiple_of(h * k, k), k)
        e_ref[0, rows, :] = expert
        g_ref[0, rows, :] = gate
        return 0

    lax.fori_loop(0, heads, head, 0)


def _peer_route(xn, wqh, wql, kh, kl, heads, nk, tm=LANES):
    t, d = xn.shape
    dk = kh.shape[-1]
    nb = t // tm
    slots = heads * PEER_TOPK
    return pl.pallas_call(
        functools.partial(_route_body, heads=heads, nk=nk, dk=dk),
        grid=(nb,),
        in_specs=[pl.BlockSpec((tm, d), lambda i: (i, 0)), _full(wqh.shape), _full(wql.shape),
                  _full(kh.shape), _full(kl.shape)],
        out_specs=[pl.BlockSpec((1, slots, tm), lambda i: (i, 0, 0))] * 2,
        out_shape=[jax.ShapeDtypeStruct((nb, slots, tm), jnp.int32),
                   jax.ShapeDtypeStruct((nb, slots, tm), F32)],
        scratch_shapes=[pltpu.VMEM((tm, wqh.shape[1]), F32)],
        compiler_params=_params(("parallel",)),
        name="peer_route",
    )(xn, wqh, wql, kh, kl)


SC_CORES = 2
SC_SUBCORES = 16
GATHER_WINDOW = 64


def _gather_rows(table, idx):
    from jax.experimental.pallas import tpu_sc as plsc
    n = idx.shape[0]
    dm = table.shape[1]
    win = GATHER_WINDOW
    workers = SC_CORES * SC_SUBCORES
    per = n // workers
    nwin = per // win
    idx3 = idx.reshape(workers, nwin, win)
    mesh = plsc.VectorSubcoreMesh(core_axis_name="c", subcore_axis_name="s")

    @functools.partial(
        pl.kernel, mesh=mesh,
        out_type=jax.ShapeDtypeStruct((n, dm), table.dtype),
        scratch_types=[pltpu.VMEM((nwin, win), jnp.int32),
                       pltpu.VMEM((win, dm), table.dtype), pltpu.VMEM((win, dm), table.dtype),
                       pltpu.SemaphoreType.DMA, pltpu.SemaphoreType.DMA,
                       pltpu.SemaphoreType.DMA, pltpu.SemaphoreType.DMA],
        name="peer_gather",
    )
    def gather(table_hbm, idx_hbm, out_hbm, idx_v, rows_a, rows_b, ga, gb, wa, wb):
        wid = lax.axis_index("s") * SC_CORES + lax.axis_index("c")
        pltpu.sync_copy(idx_hbm.at[wid], idx_v)

        @pl.loop(0, nwin, step=2)
        def _(w):
            in_a = pltpu.async_copy(table_hbm.at[idx_v.at[w]], rows_a, ga)
            in_b = pltpu.async_copy(table_hbm.at[idx_v.at[w + 1]], rows_b, gb)
            base = wid * per + w * win
            in_a.wait()
            out_a = pltpu.async_copy(rows_a, out_hbm.at[pl.ds(base, win)], wa)
            in_b.wait()
            out_b = pltpu.async_copy(rows_b, out_hbm.at[pl.ds(base + win, win)], wb)
            out_a.wait()
            out_b.wait()

    return gather(table, idx3)


EVAL_TILES = 2


def _pack_bf16_halves(tab):
    half = tab.shape[1] // 2
    bits = lax.bitcast_convert_type(tab.astype(BF16), jnp.uint16).astype(jnp.uint32)
    return bits[:, :half] | (bits[:, half:] << 16)


def _unpack_bf16_halves(w):
    lo = lax.bitcast_convert_type(w << 16, F32)
    hi = lax.bitcast_convert_type(w & jnp.uint32(0xFFFF0000), F32)
    return lo, hi


def _eval_body(ug_ref, vg_ref, xn_ref, g_ref, h_ref, fw_ref, o_ref, *, slots):
    r = SUBLANES
    half = xn_ref.shape[1] // 2
    lane = lax.broadcasted_iota(jnp.int32, (r, slots), 1)
    for tile in range(EVAL_TILES):
        rows = slice(tile * r, (tile + 1) * r)
        x_lo = xn_ref[rows, :half]
        x_hi = xn_ref[rows, half:]
        hid = jnp.zeros((r, slots), F32)
        for k in range(slots):
            u_lo, u_hi = _unpack_bf16_halves(ug_ref[tile, k * r:(k + 1) * r, :])
            hk = jnp.sum(u_lo * x_lo + u_hi * x_hi, axis=-1, keepdims=True)
            hid = jnp.where(lane == k, hk, hid)
        act = _gelu(hid) * g_ref[rows, :]
        acc_lo = jnp.zeros((r, half), F32)
        acc_hi = jnp.zeros((r, half), F32)
        for k in range(slots):
            v_lo, v_hi = _unpack_bf16_halves(vg_ref[tile, k * r:(k + 1) * r, :])
            a = act[:, k:k + 1]
            acc_lo = acc_lo + a * v_lo
            acc_hi = acc_hi + a * v_hi
        o_ref[rows, :] = _rms(h_ref[rows, :] + jnp.concatenate([acc_lo, acc_hi], axis=1), fw_ref[...])


def _peer_eval(ug, vg, xn, gates, h, fw, slots):
    t, d = xn.shape
    r = SUBLANES
    g = EVAL_TILES
    ug = ug.reshape(t // r, slots * r, d // 2)
    vg = vg.reshape(t // r, slots * r, d // 2)
    return pl.pallas_call(
        functools.partial(_eval_body, slots=slots),
        grid=(t // (r * g),),
        in_specs=[pl.BlockSpec((g, slots * r, d // 2), lambda i: (i, 0, 0)),
                  pl.BlockSpec((g, slots * r, d // 2), lambda i: (i, 0, 0)),
                  pl.BlockSpec((g * r, d), lambda i: (i, 0)), pl.BlockSpec((g * r, slots), lambda i: (i, 0)),
                  pl.BlockSpec((g * r, d), lambda i: (i, 0)), _full(fw.shape)],
        out_specs=pl.BlockSpec((g * r, d), lambda i: (i, 0)),
        out_shape=jax.ShapeDtypeStruct((t, d), F32),
        compiler_params=_params(("parallel",)),
        name="peer_eval",
    )(ug, vg, xn, gates, h, fw)


PEER_CHUNK_TOKENS = 2048


def _peer(h, xn, w_q, sub_keys, u_tab, v_tab, final_norm_w):
    t, d = h.shape
    heads, _, nk, dk = sub_keys.shape
    slots = heads * PEER_TOPK
    wqh, wql = _split_bf16(w_q.astype(F32))
    kh, kl = _split_bf16(sub_keys.reshape(heads * 2, nk, dk).astype(F32))
    expert_t, gates_t = _peer_route(xn, wqh, wql, kh, kl, heads, nk)
    nb, _, tm = expert_t.shape
    r = SUBLANES
    idx = expert_t.reshape(nb, slots, tm // r, r).transpose(0, 2, 1, 3).reshape(t // r, slots * r)
    gates = gates_t.transpose(0, 2, 1).reshape(t, slots)
    fw = final_norm_w.reshape(1, d).astype(F32)
    u_pack = _pack_bf16_halves(u_tab)
    v_pack = _pack_bf16_halves(v_tab)
    tc = min(PEER_CHUNK_TOKENS, t)
    outs = []
    for c in range(t // tc):
        rows = slice(c * tc, (c + 1) * tc)
        ci = idx[c * tc // r:(c + 1) * tc // r].reshape(-1)
        ug = _gather_rows(u_pack, ci)
        vg = _gather_rows(v_pack, ci)
        outs.append(_peer_eval(ug, vg, xn[rows], gates[rows], h[rows], fw, slots))
    return jnp.concatenate(outs, axis=0)


def kernel(x, mix_norm_w, w_in, s5_a_re, s5_a_im, s5_log_dt, s5_b_re, s5_b_im, s5_c_re, s5_c_im, s5_d, w_glu, conv_w, conv_b, ssd_dt_bias, ssd_a_log, ssd_d, ssd_norm_w, w_ssd_out, w_o, ffn_norm_w, peer_w_q, peer_sub_keys, peer_u, peer_v, final_norm_w):
    bsz, seqlen, d = x.shape
    x2 = x.reshape(bsz * seqlen, d)
    h, xn = _mixer(x2, bsz, seqlen, mix_norm_w[0], w_in[0], s5_a_re[0], s5_a_im[0], s5_log_dt[0],
                   s5_b_re[0], s5_b_im[0], s5_c_re[0], s5_c_im[0], s5_d[0], w_glu[0], conv_w[0], conv_b[0],
                   ssd_dt_bias[0], ssd_a_log[0], ssd_d[0], ssd_norm_w[0], w_ssd_out[0], w_o[0], ffn_norm_w[0])
    out = _peer(h, xn, peer_w_q[0], peer_sub_keys[0], peer_u[0].astype(F32), peer_v[0].astype(F32), final_norm_w)
    return out.reshape(bsz, seqlen, d).astype(x.dtype)
```

```python
import functools

import jax
import jax.numpy as jnp
from jax import lax
from jax.experimental import pallas as pl
from jax.experimental.pallas import tpu as pltpu

F32 = jnp.float32
BF16 = jnp.bfloat16

RMS_EPS = 1e-6
SUBLANES = 8
LANES = 128
SSD_CHUNK = 128
SSD_HEADDIM = 64
SSD_GROUPS = 4
SSD_STATE = 64
PEER_TOPK = 16
VMEM_LIMIT = 56 * 1024 * 1024


def _params(sem):
    return pltpu.CompilerParams(dimension_semantics=sem, vmem_limit_bytes=VMEM_LIMIT)


def _rms(x, w):
    return x * lax.rsqrt(jnp.mean(x * x, axis=-1, keepdims=True) + RMS_EPS) * w


def _dot(a, b):
    return jnp.dot(a, b, preferred_element_type=F32)


def _full(shape):
    n = len(shape)
    return pl.BlockSpec(shape, lambda *_: (0,) * n)


def _inproj_body(x_ref, nw_ref, wu_ref, wz_ref, wx_ref, wdf_ref, wdb_ref, wga_ref, wgb_ref,
                 u_ref, z_ref, xbc_ref, dtf_ref, dtb_ref, ga_ref, gb_ref):
    xb = _rms(x_ref[...], nw_ref[...]).astype(BF16)
    u_ref[...] = _dot(xb, wu_ref[...]).astype(BF16)
    z_ref[...] = _dot(xb, wz_ref[...])
    xbc_ref[...] = _dot(xb, wx_ref[...])
    dtf_ref[...] = _dot(xb, wdf_ref[...])
    dtb_ref[...] = _dot(xb, wdb_ref[...])
    ga_ref[...] = _dot(xb, wga_ref[...])
    gb_ref[...] = _dot(xb, wgb_ref[...])


def _in_proj(x2, nw, ws, tm=512):
    t, d = x2.shape
    widths = [w.shape[1] for w in ws]
    dts = [BF16] + [F32] * 6
    return pl.pallas_call(
        _inproj_body,
        grid=(t // tm,),
        in_specs=[pl.BlockSpec((tm, d), lambda i: (i, 0)), _full((1, d))] + [_full(w.shape) for w in ws],
        out_specs=[pl.BlockSpec((tm, n), lambda i: (i, 0)) for n in widths],
        out_shape=[jax.ShapeDtypeStruct((t, n), dt) for n, dt in zip(widths, dts)],
        compiler_params=_params(("parallel",)),
        name="in_proj",
    )(x2, nw, *ws)


S5_LANE_GROUP = 4


def _s5_body(u_ref, bd_ref, cd_ref, tab_ref, y_ref, s_ref, car_ref, *, reverse, lb, ns):
    @pl.when(pl.program_id(1) == 0)
    def _():
        car_ref[...] = jnp.zeros_like(car_ref)

    s_ref[...] = _dot(u_ref[...], bd_ref[...])
    nt = lb // SUBLANES
    edge = 0 if reverse else SUBLANES - 1

    for c0 in range(0, ns // LANES, S5_LANE_GROUP):
        los = [(c0 + g) * LANES for g in range(S5_LANE_GROUP)]

        def tile_step(i, carry, los=los):
            r = (nt - 1 - i) if reverse else i
            rows = pl.ds(pl.multiple_of(r * SUBLANES, SUBLANES), SUBLANES)
            new = []
            for g, lo in enumerate(los):
                re = s_ref[rows, lo:lo + LANES]
                im = s_ref[rows, ns + lo:ns + lo + LANES]
                for kk, k in enumerate((1, 2, 4)):
                    pr = tab_ref[2 * kk, :, lo:lo + LANES]
                    pi = tab_ref[2 * kk + 1, :, lo:lo + LANES]
                    sh = (SUBLANES - k) if reverse else k
                    rr = pltpu.roll(re, sh, 0)
                    ri = pltpu.roll(im, sh, 0)
                    re, im = re + pr * rr - pi * ri, im + pr * ri + pi * rr
                pr = tab_ref[6, :, lo:lo + LANES]
                pi = tab_ref[7, :, lo:lo + LANES]
                cr, ci = carry[2 * g], carry[2 * g + 1]
                re, im = re + pr * cr - pi * ci, im + pr * ci + pi * cr
                s_ref[rows, lo:lo + LANES] = re
                s_ref[rows, ns + lo:ns + lo + LANES] = im
                new.append(jnp.broadcast_to(re[edge:edge + 1, :], (SUBLANES, LANES)))
                new.append(jnp.broadcast_to(im[edge:edge + 1, :], (SUBLANES, LANES)))
            return tuple(new)

        carry0 = []
        for lo in los:
            carry0.append(car_ref[0, :, lo:lo + LANES])
            carry0.append(car_ref[1, :, lo:lo + LANES])
        carry = lax.fori_loop(0, nt, tile_step, tuple(carry0))
        for g, lo in enumerate(los):
            car_ref[0, :, lo:lo + LANES] = carry[2 * g]
            car_ref[1, :, lo:lo + LANES] = carry[2 * g + 1]

    y_ref[...] = _dot(s_ref[...].astype(BF16), cd_ref[...])


def _s5_dir(u, bd, cd, tab, bsz, seqlen, reverse, lb=256):
    t, w = u.shape
    ns = bd.shape[1] // 2
    nb = seqlen // lb

    def blk(b, j):
        return (b * nb + ((nb - 1 - j) if reverse else j), 0)

    return pl.pallas_call(
        functools.partial(_s5_body, reverse=reverse, lb=lb, ns=ns),
        grid=(bsz, nb),
        in_specs=[pl.BlockSpec((lb, w), blk), _full(bd.shape), _full(cd.shape), _full(tab.shape)],
        out_specs=pl.BlockSpec((lb, w), blk),
        out_shape=jax.ShapeDtypeStruct((t, w), F32),
        scratch_shapes=[pltpu.VMEM((lb, 2 * ns), F32), pltpu.VMEM((2, SUBLANES, ns), F32)],
        compiler_params=_params(("parallel", "arbitrary")),
        name="s5_bwd" if reverse else "s5_fwd",
    )(u, bd, cd, tab)


def _s5_prep(a_re, a_im, log_dt, b_re, b_im, c_re, c_im, reverse):
    g, p = a_re.shape
    c = b_re.shape[-1]
    ar = a_re.astype(F32)
    ai = a_im.astype(F32)
    step = jnp.exp(log_dt.astype(F32))[:, None]
    mag = jnp.exp(ar * step)
    lam_r = mag * jnp.cos(ai * step)
    lam_i = mag * jnp.sin(ai * step)
    den = ar * ar + ai * ai
    f_r = ((lam_r - 1.0) * ar + lam_i * ai) / den
    f_i = (lam_i * ar - (lam_r - 1.0) * ai) / den
    bt_r = f_r[..., None] * b_re - f_i[..., None] * b_im
    bt_i = f_r[..., None] * b_im + f_i[..., None] * b_re
    eye = jnp.eye(g, dtype=F32)
    bd = jnp.concatenate([jnp.einsum('gpc,gh->gchp', bt_r, eye).reshape(g * c, g * p),
                          jnp.einsum('gpc,gh->gchp', bt_i, eye).reshape(g * c, g * p)], axis=1)
    cd = jnp.concatenate([jnp.einsum('gcp,gh->gphc', c_re, eye).reshape(g * p, g * c),
                          -jnp.einsum('gcp,gh->gphc', c_im, eye).reshape(g * p, g * c)], axis=0)
    lr = lam_r.reshape(1, g * p)
    li = lam_i.reshape(1, g * p)
    pw = [(lr, li)]
    for _ in range(SUBLANES - 1):
        qr, qi = pw[-1]
        pw.append((qr * lr - qi * li, qr * li + qi * lr))
    row = jnp.arange(SUBLANES)[:, None]
    tabs = []
    for k in (1, 2, 4):
        keep = (row < SUBLANES - k) if reverse else (row >= k)
        for comp in pw[k - 1]:
            tabs.append(jnp.where(keep, comp, 0.0))
    order = list(range(SUBLANES - 1, -1, -1)) if reverse else list(range(SUBLANES))
    for comp in (0, 1):
        tabs.append(jnp.concatenate([pw[k][comp] for k in order], axis=0))
    return bd.astype(BF16), cd.astype(BF16), jnp.stack(tabs, axis=0)


def _conv_body(cur_ref, prev_ref, nxt_ref, w_ref, b_ref, xs_ref, bc_ref, *, nblk, width, d_inner):
    j = pl.program_id(0) % nblk
    pad = width // 2
    cur = cur_ref[...]
    prev = jnp.where(j == 0, 0.0, prev_ref[...])
    nxt = jnp.where(j == nblk - 1, 0.0, nxt_ref[...])
    ext = jnp.concatenate([prev, cur, nxt], axis=0)
    tm = cur.shape[0]
    acc = jnp.zeros_like(cur) + b_ref[...]
    for k in range(width):
        off = SUBLANES - pad + k
        acc = acc + ext[off:off + tm, :] * w_ref[k:k + 1, :]
    act = acc * jax.nn.sigmoid(acc)
    xs_ref[...] = act[:, :d_inner]
    bc_ref[...] = act[:, d_inner:].astype(BF16)


def _conv(xbc, w, b, seqlen, d_inner, tm=256):
    t, cdim = xbc.shape
    width = w.shape[0]
    nblk = seqlen // tm
    r = tm // SUBLANES
    last = t // SUBLANES - 1
    return pl.pallas_call(
        functools.partial(_conv_body, nblk=nblk, width=width, d_inner=d_inner),
        grid=(t // tm,),
        in_specs=[pl.BlockSpec((tm, cdim), lambda i: (i, 0)),
                  pl.BlockSpec((SUBLANES, cdim), lambda i: (jnp.maximum(i * r - 1, 0), 0)),
                  pl.BlockSpec((SUBLANES, cdim), lambda i: (jnp.minimum((i + 1) * r, last), 0)),
                  _full(w.shape), _full(b.shape)],
        out_specs=[pl.BlockSpec((tm, d_inner), lambda i: (i, 0)),
                   pl.BlockSpec((tm, cdim - d_inner), lambda i: (i, 0))],
        out_shape=[jax.ShapeDtypeStruct((t, d_inner), F32),
                   jax.ShapeDtypeStruct((t, cdim - d_inner), BF16)],
        compiler_params=_params(("parallel",)),
        name="ssd_conv",
    )(xbc, xbc, xbc, w, b)


def _cumsum_rows(x):
    n = x.shape[0]
    ri = lax.broadcasted_iota(jnp.int32, x.shape, 0)
    k = 1
    while k < n:
        x = x + jnp.where(ri >= k, pltpu.roll(x, k, 0), 0.0)
        k *= 2
    return x


def _softplus(x):
    return jnp.maximum(x, 0.0) + jnp.log1p(jnp.exp(-jnp.abs(x)))


def _ssd_body(xs_ref, bc_ref, dt_ref, bias_ref, a_ref, y_ref, st_ref, *, reverse, heads):
    @pl.when(pl.program_id(1) == 0)
    def _():
        st_ref[...] = jnp.zeros_like(st_ref)

    n = SSD_CHUNK
    p = SSD_HEADDIM
    ng = SSD_GROUPS
    ns = SSD_STATE
    hg = heads // ng
    dt = _softplus(dt_ref[...] + bias_ref[...])
    adt = dt * a_ref[...]
    cs = _cumsum_rows(adt)
    tot = cs[n - 1:n, :]
    if reverse:
        cs = tot - cs + adt
    cs_t = cs.T
    dt_t = dt.T
    ecs = jnp.exp(cs)
    dst = jnp.exp(tot - cs) * dt
    etot = jnp.exp(tot)
    li = lax.broadcasted_iota(jnp.int32, (n, n), 0)
    si = lax.broadcasted_iota(jnp.int32, (n, n), 1)
    mask = (si >= li) if reverse else (si <= li)
    for g in range(ng):
        bg = bc_ref[:, g * ns:(g + 1) * ns]
        cg = bc_ref[:, ng * ns + g * ns:ng * ns + (g + 1) * ns]
        cb = lax.dot_general(cg, bg, (((1,), (1,)), ((), ())), preferred_element_type=F32)
        for hh in range(hg):
            h = g * hg + hh
            xh = xs_ref[:, h * p:(h + 1) * p]
            lm = jnp.exp(jnp.where(mask, cs[:, h:h + 1] - cs_t[h:h + 1, :], -jnp.inf))
            m = (cb * lm * dt_t[h:h + 1, :]).astype(BF16)
            yd = _dot(m, xh.astype(BF16))
            sp = st_ref[h]
            yo = _dot(cg, sp.astype(BF16)) * ecs[:, h:h + 1]
            y_ref[:, h * p:(h + 1) * p] = yd + yo
            xw = (xh * dst[:, h:h + 1]).astype(BF16)
            stn = lax.dot_general(bg, xw, (((0,), (0,)), ((), ())), preferred_element_type=F32)
            st_ref[h] = sp * etot[:, h:h + 1] + stn


def _ssd_dir(xs, bc, dt, bias, a, bsz, seqlen, heads, reverse):
    t, d_inner = xs.shape
    n = SSD_CHUNK
    nc = seqlen // n

    def blk(b, j):
        return (b * nc + ((nc - 1 - j) if reverse else j), 0)

    return pl.pallas_call(
        functools.partial(_ssd_body, reverse=reverse, heads=heads),
        grid=(bsz, nc),
        in_specs=[pl.BlockSpec((n, d_inner), blk), pl.BlockSpec((n, bc.shape[1]), blk),
                  pl.BlockSpec((n, LANES), blk), _full(bias.shape), _full(a.shape)],
        out_specs=pl.BlockSpec((n, d_inner), blk),
        out_shape=jax.ShapeDtypeStruct((t, d_inner), F32),
        scratch_shapes=[pltpu.VMEM((heads, SSD_STATE, SSD_HEADDIM), F32)],
        compiler_params=_params(("parallel", "arbitrary")),
        name="ssd_bwd" if reverse else "ssd_fwd",
    )(xs, bc, dt, bias, a)


def _gelu(x):
    return 0.5 * x * (1.0 + lax.erf(x * (2.0 ** -0.5)))


def _merge_body(x_ref, u_ref, y5f_ref, y5b_ref, d5_ref, wglu_ref, ga_ref,
                xs_ref, ysf_ref, ysb_ref, dss_ref, z_ref, nw_ref, wso_ref, gb_ref, wo_ref, fnw_ref,
                h_ref, xn_ref, *, d_model, groups):
    ya = u_ref[...].astype(F32) * d5_ref[...] + y5f_ref[...] + y5b_ref[...]
    glu = _dot(_gelu(ya).astype(BF16), wglu_ref[...])
    ya = glu[:, :d_model] * jax.nn.sigmoid(glu[:, d_model:])
    z = z_ref[...]
    yb = (xs_ref[...] * dss_ref[...] + ysf_ref[...] + ysb_ref[...]) * (z * jax.nn.sigmoid(z))
    gw = yb.shape[1] // groups
    parts = []
    for g in range(groups):
        yg = yb[:, g * gw:(g + 1) * gw]
        parts.append(yg * lax.rsqrt(jnp.mean(yg * yg, axis=-1, keepdims=True) + RMS_EPS))
    yb = jnp.concatenate(parts, axis=1) * nw_ref[...]
    yb = _dot(yb.astype(BF16), wso_ref[...])
    merged = jax.nn.sigmoid(ga_ref[...]) * ya + jax.nn.sigmoid(gb_ref[...]) * yb
    h = x_ref[...] + _dot(merged.astype(BF16), wo_ref[...])
    h_ref[...] = h
    xn_ref[...] = _rms(h, fnw_ref[...])


def _merge(x2, u, y5f, y5b, d5, wglu, ga, xs, ysf, ysb, dss, z, nw, wso, gb, wo, fnw, tm=256):
    t, d = x2.shape
    args = [x2, u, y5f, y5b, d5, wglu, ga, xs, ysf, ysb, dss, z, nw, wso, gb, wo, fnw]
    per_token = [True, True, True, True, False, False, True, True, True, True, False, True, False, False, True,
                 False, False]
    specs = []
    for a, tok in zip(args, per_token):
        if tok:
            specs.append(pl.BlockSpec((tm, a.shape[1]), lambda i: (i, 0)))
        else:
            specs.append(_full(a.shape))
    return pl.pallas_call(
        functools.partial(_merge_body, d_model=d, groups=SSD_GROUPS),
        grid=(t // tm,),
        in_specs=specs,
        out_specs=[pl.BlockSpec((tm, d), lambda i: (i, 0))] * 2,
        out_shape=[jax.ShapeDtypeStruct((t, d), F32)] * 2,
        compiler_params=_params(("parallel",)),
        name="merge",
    )(*args)


def _mixer(x2, bsz, seqlen, mix_norm_w, w_in, s5_a_re, s5_a_im, s5_log_dt, s5_b_re, s5_b_im, s5_c_re, s5_c_im,
           s5_d, w_glu, conv_w, conv_b, ssd_dt_bias, ssd_a_log, ssd_d, ssd_norm_w, w_ssd_out, w_o, ffn_norm_w):
    t, d = x2.shape
    s5w = s5_d.shape[0]
    d_inner = w_ssd_out.shape[0]
    heads = ssd_d.shape[0]
    cdim = conv_b.shape[0]
    o1 = s5w
    o2 = o1 + d_inner
    o3 = o2 + cdim
    o4 = o3 + 2 * heads

    def padw(w):
        return jnp.pad(w, ((0, 0), (0, LANES - w.shape[1])))

    ws = [w_in[:, :o1], w_in[:, o1:o2], w_in[:, o2:o3], padw(w_in[:, o3:o3 + heads]),
          padw(w_in[:, o3 + heads:o4]), w_in[:, o4:o4 + d], w_in[:, o4 + d:]]
    ws = [w.astype(BF16) for w in ws]
    u, z, xbc, dtf, dtb, ga, gb = _in_proj(x2, mix_norm_w.reshape(1, d), ws)

    y5 = []
    for rev in (False, True):
        k = int(rev)
        bd, cd, tab = _s5_prep(s5_a_re[k], s5_a_im[k], s5_log_dt[k], s5_b_re[k], s5_b_im[k],
                               s5_c_re[k], s5_c_im[k], rev)
        y5.append(_s5_dir(u, bd, cd, tab, bsz, seqlen, rev))

    xs, bc = _conv(xbc, conv_w.reshape(conv_w.shape[0], cdim), conv_b.reshape(1, cdim), seqlen, d_inner)
    ys = []
    for rev in (False, True):
        k = int(rev)
        bias = jnp.pad(ssd_dt_bias[k].astype(F32), (0, LANES - heads)).reshape(1, LANES)
        a = jnp.pad(-jnp.exp(ssd_a_log[k].astype(F32)), (0, LANES - heads)).reshape(1, LANES)
        ys.append(_ssd_dir(xs, bc, dtb if rev else dtf, bias, a, bsz, seqlen, heads, rev))

    dss = jnp.repeat(ssd_d.astype(F32), d_inner // heads).reshape(1, d_inner)
    return _merge(x2, u, y5[0], y5[1], s5_d.reshape(1, s5w).astype(F32), w_glu.astype(BF16), ga,
                  xs, ys[0], ys[1], dss, z, ssd_norm_w.reshape(1, d_inner).astype(F32),
                  w_ssd_out.astype(BF16), gb, w_o.astype(BF16), ffn_norm_w.reshape(1, d).astype(F32))


def _split_bf16(x):
    hi = x.astype(BF16)
    return hi, (x - hi.astype(F32)).astype(BF16)


def _dot3(ah, al, bh, bl, dims=(((1,), (0,)), ((), ()))):
    def dg(p, q):
        return lax.dot_general(p, q, dims, preferred_element_type=F32)
    return dg(ah, bh) + (dg(al, bh) + dg(ah, bl))


def _topk_rows(xs, k, payloads=None):
    n = xs[0].shape[0]
    ri = lax.broadcasted_iota(jnp.int32, xs[0].shape, 0)
    xs = list(xs)
    vals = [[] for _ in xs]
    outs = [[] for _ in xs]
    for _ in range(k):
        for i, x in enumerate(xs):
            m = jnp.max(x, axis=0, keepdims=True)
            pos = jnp.min(jnp.where(x == m, ri, n), axis=0, keepdims=True)
            sel = ri == pos
            vals[i].append(m)
            if payloads is None:
                outs[i].append(pos)
            else:
                outs[i].append(jnp.max(jnp.where(sel, payloads[i], -1), axis=0, keepdims=True))
            xs[i] = jnp.where(sel, -jnp.inf, x)
    return [(jnp.concatenate(v, axis=0), jnp.concatenate(o, axis=0)) for v, o in zip(vals, outs)]


def _pair_candidates(c1, c2, combine):
    r = SUBLANES
    top = lax.broadcasted_iota(jnp.int32, (r, c1.shape[1]), 0) < r // 2
    lo = c2[0:r]
    quad = jnp.concatenate([c2[0:r // 2], c2[0:r // 2]], axis=0)
    blocks = [combine(c1[0:1], lo), combine(c1[0:1], c2[r:2 * r])]
    blocks += [combine(c1[a:a + 1], lo) for a in (1, 2, 3)]
    blocks += [combine(jnp.where(top, c1[a:a + 1], c1[a + 1:a + 2]), quad) for a in (4, 6)]
    blocks.append(combine(c1[r:2 * r], c2[0:1]))
    return jnp.concatenate(blocks, axis=0)


def _route_body(xn_ref, wqh_ref, wql_ref, kh_ref, kl_ref, e_ref, g_ref, q_ref, *, heads, nk, dk):
    k = PEER_TOPK
    xh, xl = _split_bf16(xn_ref[...])
    q_ref[...] = _dot3(xh, xl, wqh_ref[...], wql_ref[...])
    nt = (((1,), (1,)), ((), ()))

    def head(h, _):
        scores = []
        for s in range(2):
            col = pl.multiple_of((h * 2 + s) * dk, dk)
            qh, ql = _split_bf16(q_ref[:, pl.ds(col, dk)])
            scores.append(_dot3(kh_ref[h * 2 + s], kl_ref[h * 2 + s], qh, ql, nt))
        (s1, i1), (s2, i2) = _topk_rows(scores, k)
        cand = _pair_candidates(s1, s2, lambda a, b: a + b)
        ids = _pair_candidates(i1, i2, lambda a, b: a * nk + b)
        (best, expert), = _topk_rows([cand], k, [ids])
        w = jnp.exp(best - best[0:1, :])
        gate = w / jnp.sum(w, axis=0, keepdims=True)
        rows = pl.ds(pl.multiple_of(h * k, k), k)
        e_ref[0, rows, :] = expert
        g_ref[0, rows, :] = gate
        return 0

    lax.fori_loop(0, heads, head, 0)


def _peer_route(xn, wqh, wql, kh, kl, heads, nk, tm=LANES):
    t, d = xn.shape
    dk = kh.shape[-1]
    nb = t // tm
    slots = heads * PEER_TOPK
    return pl.pallas_call(
        functools.partial(_route_body, heads=heads, nk=nk, dk=dk),
        grid=(nb,),
        in_specs=[pl.BlockSpec((tm, d), lambda i: (i, 0)), _full(wqh.shape), _full(wql.shape),
                  _full(kh.shape), _full(kl.shape)],
        out_specs=[pl.BlockSpec((1, slots, tm), lambda i: (i, 0, 0))] * 2,
        out_shape=[jax.ShapeDtypeStruct((nb, slots, tm), jnp.int32),
                   jax.ShapeDtypeStruct((nb, slots, tm), F32)],
        scratch_shapes=[pltpu.VMEM((tm, wqh.shape[1]), F32)],
        compiler_params=_params(("parallel",)),
        name="peer_route",
    )(xn, wqh, wql, kh, kl)


SC_CORES = 2
SC_SUBCORES = 16
GATHER_WINDOW = 64


def _gather_rows(table, idx):
    from jax.experimental.pallas import tpu_sc as plsc
    n = idx.shape[0]
    dm = table.shape[1]
    win = GATHER_WINDOW
    workers = SC_CORES * SC_SUBCORES
    per = n // workers
    nwin = per // win
    idx3 = idx.reshape(workers, nwin, win)
    mesh = plsc.VectorSubcoreMesh(core_axis_name="c", subcore_axis_name="s")

    @functools.partial(
        pl.kernel, mesh=mesh,
        out_type=jax.ShapeDtypeStruct((n, dm), table.dtype),
        scratch_types=[pltpu.VMEM((nwin, win), jnp.int32),
                       pltpu.VMEM((win, dm), table.dtype), pltpu.VMEM((win, dm), table.dtype),
                       pltpu.SemaphoreType.DMA, pltpu.SemaphoreType.DMA,
                       pltpu.SemaphoreType.DMA, pltpu.SemaphoreType.DMA],
        name="peer_gather",
    )
    def gather(table_hbm, idx_hbm, out_hbm, idx_v, rows_a, rows_b, ga, gb, wa, wb):
        wid = lax.axis_index("s") * SC_CORES + lax.axis_index("c")
        pltpu.sync_copy(idx_hbm.at[wid], idx_v)

        @pl.loop(0, nwin, step=2)
        def _(w):
            in_a = pltpu.async_copy(table_hbm.at[idx_v.at[w]], rows_a, ga)
            in_b = pltpu.async_copy(table_hbm.at[idx_v.at[w + 1]], rows_b, gb)
            base = wid * per + w * win
            in_a.wait()
            out_a = pltpu.async_copy(rows_a, out_hbm.at[pl.ds(base, win)], wa)
            in_b.wait()
            out_b = pltpu.async_copy(rows_b, out_hbm.at[pl.ds(base + win, win)], wb)
            out_a.wait()
            out_b.wait()

    return gather(table, idx3)


SC_LANES = 16
HID_UNIT = 32


def _sc_hidden(u_pack, idx_tok, xn):
    from jax.experimental.pallas import tpu_sc as plsc
    t, slots = idx_tok.shape
    half = u_pack.shape[1]
    d = xn.shape[1]
    workers = SC_CORES * SC_SUBCORES
    tpw = t // workers
    upt = slots // HID_UNIT
    nl = SC_LANES
    idx3 = idx_tok.reshape(workers, tpw * upt, HID_UNIT)
    mesh = plsc.VectorSubcoreMesh(core_axis_name="c", subcore_axis_name="s")

    @functools.partial(
        pl.kernel, mesh=mesh,
        out_type=jax.ShapeDtypeStruct((t, slots), F32),
        scratch_types=[pltpu.VMEM((tpw * upt, HID_UNIT), jnp.int32),
                       pltpu.VMEM((HID_UNIT, half), jnp.uint32), pltpu.VMEM((HID_UNIT, half), jnp.uint32),
                       pltpu.VMEM((d,), F32), pltpu.VMEM((d,), F32),
                       pltpu.VMEM((tpw, slots), F32),
                       pltpu.SemaphoreType.DMA, pltpu.SemaphoreType.DMA,
                       pltpu.SemaphoreType.DMA, pltpu.SemaphoreType.DMA],
        compiler_params=pltpu.CompilerParams(needs_layout_passes=False),
        name="peer_hidden",
    )
    def hidden(u_hbm, idx_hbm, x_hbm, hid_hbm, idx_v, rows_a, rows_b, x_a, x_b, hid_v, ga, gb, xa, xb):
        wid = lax.axis_index("s") * SC_CORES + lax.axis_index("c")
        tok0 = wid * tpw
        pltpu.sync_copy(idx_hbm.at[wid], idx_v)
        bufs = ((rows_a, ga), (rows_b, gb))
        xbufs = ((x_a, xa), (x_b, xb))

        def rows_copy(unit, which):
            buf, sem = bufs[which]
            return pltpu.make_async_copy(u_hbm.at[idx_v.at[unit]], buf, sem)

        def x_copy(tok, which):
            buf, sem = xbufs[which]
            return pltpu.make_async_copy(x_hbm.at[tok0 + tok], buf, sem)

        def compute(rows_ref, x_ref, tok, q):
            lane = lax.iota(jnp.int32, nl)
            for grp in range(HID_UNIT // nl):
                def body(j, acc, grp=grp):
                    off = pl.multiple_of(j * nl, nl)
                    x_lo = x_ref[pl.ds(off, nl)]
                    x_hi = x_ref[pl.ds(half + off, nl)]
                    new = []
                    for r in range(nl):
                        w = rows_ref[grp * nl + r, pl.ds(off, nl)]
                        lo = lax.bitcast_convert_type(w << 16, F32)
                        hi = lax.bitcast_convert_type(w & jnp.uint32(0xFFFF0000), F32)
                        new.append(acc[r] + lo * x_lo + hi * x_hi)
                    return tuple(new)

                acc = lax.fori_loop(0, half // nl, body, tuple(jnp.zeros((nl,), F32) for _ in range(nl)))
                res = jnp.zeros((nl,), F32)
                for r in range(nl):
                    res = jnp.where(lane == r, jnp.sum(acc[r]), res)
                hid_v[tok, pl.ds(q * HID_UNIT + grp * nl, nl)] = res

        rows_copy(0, 0).start()
        x_copy(0, 0).start()

        @pl.loop(0, tpw, step=2)
        def _(t0):
            for tt in range(2):
                tok = t0 + tt
                x_copy(tok, tt).wait()

                @pl.when(tok + 1 < tpw)
                def _():
                    x_copy(tok + 1, 1 - tt).start()

                for q in range(upt):
                    unit = tok * upt + q
                    if q + 1 < upt:
                        rows_copy(unit + 1, (q + 1) % 2).start()
                    else:
                        @pl.when(tok + 1 < tpw)
                        def _():
                            rows_copy(unit + 1, (q + 1) % 2).start()
                    rows_copy(unit, q % 2).wait()
                    compute(bufs[q % 2][0], xbufs[tt][0], tok, q)

        pltpu.sync_copy(hid_v, hid_hbm.at[pl.ds(tok0, tpw)])

    return hidden(u_pack, idx3, xn)


EVAL_TILES = 2


def _pack_bf16_halves(tab):
    half = tab.shape[1] // 2
    bits = lax.bitcast_convert_type(tab.astype(BF16), jnp.uint16).astype(jnp.uint32)
    return bits[:, :half] | (bits[:, half:] << 16)


def _unpack_bf16_halves(w):
    lo = lax.bitcast_convert_type(w << 16, F32)
    hi = lax.bitcast_convert_type(w & jnp.uint32(0xFFFF0000), F32)
    return lo, hi


def _eval_body(vg_ref, hid_ref, g_ref, h_ref, fw_ref, o_ref, *, slots):
    r = SUBLANES
    half = h_ref.shape[1] // 2
    for tile in range(EVAL_TILES):
        rows = slice(tile * r, (tile + 1) * r)
        act = _gelu(hid_ref[rows, :]) * g_ref[rows, :]
        acc_lo = jnp.zeros((r, half), F32)
        acc_hi = jnp.zeros((r, half), F32)
        for k in range(slots):
            v_lo, v_hi = _unpack_bf16_halves(vg_ref[tile, k * r:(k + 1) * r, :])
            a = act[:, k:k + 1]
            acc_lo = acc_lo + a * v_lo
            acc_hi = acc_hi + a * v_hi
        o_ref[rows, :] = _rms(h_ref[rows, :] + jnp.concatenate([acc_lo, acc_hi], axis=1), fw_ref[...])


def _peer_eval(vg, hid, gates, h, fw, slots):
    t, d = h.shape
    r = SUBLANES
    g = EVAL_TILES
    vg = vg.reshape(t // r, slots * r, d // 2)
    return pl.pallas_call(
        functools.partial(_eval_body, slots=slots),
        grid=(t // (r * g),),
        in_specs=[pl.BlockSpec((g, slots * r, d // 2), lambda i: (i, 0, 0)),
                  pl.BlockSpec((g * r, slots), lambda i: (i, 0)), pl.BlockSpec((g * r, slots), lambda i: (i, 0)),
                  pl.BlockSpec((g * r, d), lambda i: (i, 0)), _full(fw.shape)],
        out_specs=pl.BlockSpec((g * r, d), lambda i: (i, 0)),
        out_shape=jax.ShapeDtypeStruct((t, d), F32),
        compiler_params=_params(("parallel",)),
        name="peer_eval",
    )(vg, hid, gates, h, fw)


PEER_CHUNK_TOKENS = 2048


def _peer(h, xn, w_q, sub_keys, u_tab, v_tab, final_norm_w):
    t, d = h.shape
    heads, _, nk, dk = sub_keys.shape
    slots = heads * PEER_TOPK
    wqh, wql = _split_bf16(w_q.astype(F32))
    kh, kl = _split_bf16(sub_keys.reshape(heads * 2, nk, dk).astype(F32))
    expert_t, gates_t = _peer_route(xn, wqh, wql, kh, kl, heads, nk)
    nb, _, tm = expert_t.shape
    r = SUBLANES
    idx = expert_t.reshape(nb, slots, tm // r, r).transpose(0, 2, 1, 3).reshape(t // r, slots * r)
    idx_tok = expert_t.transpose(0, 2, 1).reshape(t, slots)
    gates = gates_t.transpose(0, 2, 1).reshape(t, slots)
    fw = final_norm_w.reshape(1, d).astype(F32)
    u_pack = _pack_bf16_halves(u_tab)
    v_pack = _pack_bf16_halves(v_tab)
    tc = min(PEER_CHUNK_TOKENS, t)
    outs = []
    for c in range(t // tc):
        rows = slice(c * tc, (c + 1) * tc)
        hid = _sc_hidden(u_pack, idx_tok[rows], xn[rows])
        vg = _gather_rows(v_pack, idx[c * tc // r:(c + 1) * tc // r].reshape(-1))
        outs.append(_peer_eval(vg, hid, gates[rows], h[rows], fw, slots))
    return jnp.concatenate(outs, axis=0)


def kernel(x, mix_norm_w, w_in, s5_a_re, s5_a_im, s5_log_dt, s5_b_re, s5_b_im, s5_c_re, s5_c_im, s5_d, w_glu, conv_w, conv_b, ssd_dt_bias, ssd_a_log, ssd_d, ssd_norm_w, w_ssd_out, w_o, ffn_norm_w, peer_w_q, peer_sub_keys, peer_u, peer_v, final_norm_w):
    bsz, seqlen, d = x.shape
    x2 = x.reshape(bsz * seqlen, d)
    h, xn = _mixer(x2, bsz, seqlen, mix_norm_w[0], w_in[0], s5_a_re[0], s5_a_im[0], s5_log_dt[0],
                   s5_b_re[0], s5_b_im[0], s5_c_re[0], s5_c_im[0], s5_d[0], w_glu[0], conv_w[0], conv_b[0],
                   ssd_dt_bias[0], ssd_a_log[0], ssd_d[0], ssd_norm_w[0], w_ssd_out[0], w_o[0], ffn_norm_w[0])
    out = _peer(h, xn, peer_w_q[0], peer_sub_keys[0], peer_u[0].astype(F32), peer_v[0].astype(F32), final_norm_w)
    return out.reshape(bsz, seqlen, d).astype(x.dtype)
```

```python
import functools

import jax
import jax.numpy as jnp
from jax import lax
from jax.experimental import pallas as pl
from jax.experimental.pallas import tpu as pltpu

F32 = jnp.float32
BF16 = jnp.bfloat16

RMS_EPS = 1e-6
SUBLANES = 8
LANES = 128
SSD_CHUNK = 128
SSD_HEADDIM = 64
SSD_GROUPS = 4
SSD_STATE = 64
PEER_TOPK = 16
VMEM_LIMIT = 56 * 1024 * 1024


def _params(sem):
    return pltpu.CompilerParams(dimension_semantics=sem, vmem_limit_bytes=VMEM_LIMIT)


def _rms(x, w):
    return x * lax.rsqrt(jnp.mean(x * x, axis=-1, keepdims=True) + RMS_EPS) * w


def _dot(a, b):
    return jnp.dot(a, b, preferred_element_type=F32)


def _full(shape):
    n = len(shape)
    return pl.BlockSpec(shape, lambda *_: (0,) * n)


def _inproj_body(x_ref, nw_ref, wu_ref, wz_ref, wx_ref, wdf_ref, wdb_ref, wga_ref, wgb_ref,
                 u_ref, z_ref, xbc_ref, dtf_ref, dtb_ref, ga_ref, gb_ref):
    xb = _rms(x_ref[...], nw_ref[...]).astype(BF16)
    u_ref[...] = _dot(xb, wu_ref[...]).astype(BF16)
    z_ref[...] = _dot(xb, wz_ref[...])
    xbc_ref[...] = _dot(xb, wx_ref[...])
    dtf_ref[...] = _dot(xb, wdf_ref[...])
    dtb_ref[...] = _dot(xb, wdb_ref[...])
    ga_ref[...] = _dot(xb, wga_ref[...])
    gb_ref[...] = _dot(xb, wgb_ref[...])


def _in_proj(x2, nw, ws, tm=512):
    t, d = x2.shape
    widths = [w.shape[1] for w in ws]
    dts = [BF16] + [F32] * 6
    return pl.pallas_call(
        _inproj_body,
        grid=(t // tm,),
        in_specs=[pl.BlockSpec((tm, d), lambda i: (i, 0)), _full((1, d))] + [_full(w.shape) for w in ws],
        out_specs=[pl.BlockSpec((tm, n), lambda i: (i, 0)) for n in widths],
        out_shape=[jax.ShapeDtypeStruct((t, n), dt) for n, dt in zip(widths, dts)],
        compiler_params=_params(("parallel",)),
        name="in_proj",
    )(x2, nw, *ws)


S5_LANE_GROUP = 4


def _s5_body(u_ref, bd_ref, cd_ref, tab_ref, y_ref, s_ref, car_ref, *, reverse, lb, ns):
    @pl.when(pl.program_id(1) == 0)
    def _():
        car_ref[...] = jnp.zeros_like(car_ref)

    s_ref[...] = _dot(u_ref[...], bd_ref[...])
    nt = lb // SUBLANES
    edge = 0 if reverse else SUBLANES - 1

    for c0 in range(0, ns // LANES, S5_LANE_GROUP):
        los = [(c0 + g) * LANES for g in range(S5_LANE_GROUP)]

        def tile_step(i, carry, los=los):
            r = (nt - 1 - i) if reverse else i
            rows = pl.ds(pl.multiple_of(r * SUBLANES, SUBLANES), SUBLANES)
            new = []
            for g, lo in enumerate(los):
                re = s_ref[rows, lo:lo + LANES]
                im = s_ref[rows, ns + lo:ns + lo + LANES]
                for kk, k in enumerate((1, 2, 4)):
                    pr = tab_ref[2 * kk, :, lo:lo + LANES]
                    pi = tab_ref[2 * kk + 1, :, lo:lo + LANES]
                    sh = (SUBLANES - k) if reverse else k
                    rr = pltpu.roll(re, sh, 0)
                    ri = pltpu.roll(im, sh, 0)
                    re, im = re + pr * rr - pi * ri, im + pr * ri + pi * rr
                pr = tab_ref[6, :, lo:lo + LANES]
                pi = tab_ref[7, :, lo:lo + LANES]
                cr, ci = carry[2 * g], carry[2 * g + 1]
                re, im = re + pr * cr - pi * ci, im + pr * ci + pi * cr
                s_ref[rows, lo:lo + LANES] = re
                s_ref[rows, ns + lo:ns + lo + LANES] = im
                new.append(jnp.broadcast_to(re[edge:edge + 1, :], (SUBLANES, LANES)))
                new.append(jnp.broadcast_to(im[edge:edge + 1, :], (SUBLANES, LANES)))
            return tuple(new)

        carry0 = []
        for lo in los:
            carry0.append(car_ref[0, :, lo:lo + LANES])
            carry0.append(car_ref[1, :, lo:lo + LANES])
        carry = lax.fori_loop(0, nt, tile_step, tuple(carry0))
        for g, lo in enumerate(los):
            car_ref[0, :, lo:lo + LANES] = carry[2 * g]
            car_ref[1, :, lo:lo + LANES] = carry[2 * g + 1]

    y_ref[...] = _dot(s_ref[...].astype(BF16), cd_ref[...])


def _s5_dir(u, bd, cd, tab, bsz, seqlen, reverse, lb=256):
    t, w = u.shape
    ns = bd.shape[1] // 2
    nb = seqlen // lb

    def blk(b, j):
        return (b * nb + ((nb - 1 - j) if reverse else j), 0)

    return pl.pallas_call(
        functools.partial(_s5_body, reverse=reverse, lb=lb, ns=ns),
        grid=(bsz, nb),
        in_specs=[pl.BlockSpec((lb, w), blk), _full(bd.shape), _full(cd.shape), _full(tab.shape)],
        out_specs=pl.BlockSpec((lb, w), blk),
        out_shape=jax.ShapeDtypeStruct((t, w), F32),
        scratch_shapes=[pltpu.VMEM((lb, 2 * ns), F32), pltpu.VMEM((2, SUBLANES, ns), F32)],
        compiler_params=_params(("parallel", "arbitrary")),
        name="s5_bwd" if reverse else "s5_fwd",
    )(u, bd, cd, tab)


def _s5_prep(a_re, a_im, log_dt, b_re, b_im, c_re, c_im, reverse):
    g, p = a_re.shape
    c = b_re.shape[-1]
    ar = a_re.astype(F32)
    ai = a_im.astype(F32)
    step = jnp.exp(log_dt.astype(F32))[:, None]
    mag = jnp.exp(ar * step)
    lam_r = mag * jnp.cos(ai * step)
    lam_i = mag * jnp.sin(ai * step)
    den = ar * ar + ai * ai
    f_r = ((lam_r - 1.0) * ar + lam_i * ai) / den
    f_i = (lam_i * ar - (lam_r - 1.0) * ai) / den
    bt_r = f_r[..., None] * b_re - f_i[..., None] * b_im
    bt_i = f_r[..., None] * b_im + f_i[..., None] * b_re
    eye = jnp.eye(g, dtype=F32)
    bd = jnp.concatenate([jnp.einsum('gpc,gh->gchp', bt_r, eye).reshape(g * c, g * p),
                          jnp.einsum('gpc,gh->gchp', bt_i, eye).reshape(g * c, g * p)], axis=1)
    cd = jnp.concatenate([jnp.einsum('gcp,gh->gphc', c_re, eye).reshape(g * p, g * c),
                          -jnp.einsum('gcp,gh->gphc', c_im, eye).reshape(g * p, g * c)], axis=0)
    lr = lam_r.reshape(1, g * p)
    li = lam_i.reshape(1, g * p)
    pw = [(lr, li)]
    for _ in range(SUBLANES - 1):
        qr, qi = pw[-1]
        pw.append((qr * lr - qi * li, qr * li + qi * lr))
    row = jnp.arange(SUBLANES)[:, None]
    tabs = []
    for k in (1, 2, 4):
        keep = (row < SUBLANES - k) if reverse else (row >= k)
        for comp in pw[k - 1]:
            tabs.append(jnp.where(keep, comp, 0.0))
    order = list(range(SUBLANES - 1, -1, -1)) if reverse else list(range(SUBLANES))
    for comp in (0, 1):
        tabs.append(jnp.concatenate([pw[k][comp] for k in order], axis=0))
    return bd.astype(BF16), cd.astype(BF16), jnp.stack(tabs, axis=0)


def _conv_body(cur_ref, prev_ref, nxt_ref, w_ref, b_ref, xs_ref, bc_ref, *, nblk, width, d_inner):
    j = pl.program_id(0) % nblk
    pad = width // 2
    cur = cur_ref[...]
    prev = jnp.where(j == 0, 0.0, prev_ref[...])
    nxt = jnp.where(j == nblk - 1, 0.0, nxt_ref[...])
    ext = jnp.concatenate([prev, cur, nxt], axis=0)
    tm = cur.shape[0]
    acc = jnp.zeros_like(cur) + b_ref[...]
    for k in range(width):
        off = SUBLANES - pad + k
        acc = acc + ext[off:off + tm, :] * w_ref[k:k + 1, :]
    act = acc * jax.nn.sigmoid(acc)
    xs_ref[...] = act[:, :d_inner]
    bc_ref[...] = act[:, d_inner:].astype(BF16)


def _conv(xbc, w, b, seqlen, d_inner, tm=256):
    t, cdim = xbc.shape
    width = w.shape[0]
    nblk = seqlen // tm
    r = tm // SUBLANES
    last = t // SUBLANES - 1
    return pl.pallas_call(
        functools.partial(_conv_body, nblk=nblk, width=width, d_inner=d_inner),
        grid=(t // tm,),
        in_specs=[pl.BlockSpec((tm, cdim), lambda i: (i, 0)),
                  pl.BlockSpec((SUBLANES, cdim), lambda i: (jnp.maximum(i * r - 1, 0), 0)),
                  pl.BlockSpec((SUBLANES, cdim), lambda i: (jnp.minimum((i + 1) * r, last), 0)),
                  _full(w.shape), _full(b.shape)],
        out_specs=[pl.BlockSpec((tm, d_inner), lambda i: (i, 0)),
                   pl.BlockSpec((tm, cdim - d_inner), lambda i: (i, 0))],
        out_shape=[jax.ShapeDtypeStruct((t, d_inner), F32),
                   jax.ShapeDtypeStruct((t, cdim - d_inner), BF16)],
        compiler_params=_params(("parallel",)),
        name="ssd_conv",
    )(xbc, xbc, xbc, w, b)


def _cumsum_rows(x):
    n = x.shape[0]
    ri = lax.broadcasted_iota(jnp.int32, x.shape, 0)
    k = 1
    while k < n:
        x = x + jnp.where(ri >= k, pltpu.roll(x, k, 0), 0.0)
        k *= 2
    return x


def _softplus(x):
    return jnp.maximum(x, 0.0) + jnp.log1p(jnp.exp(-jnp.abs(x)))


def _ssd_body(xs_ref, bc_ref, dt_ref, bias_ref, a_ref, y_ref, st_ref, *, reverse, heads):
    @pl.when(pl.program_id(1) == 0)
    def _():
        st_ref[...] = jnp.zeros_like(st_ref)

    n = SSD_CHUNK
    p = SSD_HEADDIM
    ng = SSD_GROUPS
    ns = SSD_STATE
    hg = heads // ng
    dt = _softplus(dt_ref[...] + bias_ref[...])
    adt = dt * a_ref[...]
    cs = _cumsum_rows(adt)
    tot = cs[n - 1:n, :]
    if reverse:
        cs = tot - cs + adt
    cs_t = cs.T
    dt_t = dt.T
    ecs = jnp.exp(cs)
    dst = jnp.exp(tot - cs) * dt
    etot = jnp.exp(tot)
    li = lax.broadcasted_iota(jnp.int32, (n, n), 0)
    si = lax.broadcasted_iota(jnp.int32, (n, n), 1)
    mask = (si >= li) if reverse else (si <= li)
    for g in range(ng):
        bg = bc_ref[:, g * ns:(g + 1) * ns]
        cg = bc_ref[:, ng * ns + g * ns:ng * ns + (g + 1) * ns]
        cb = lax.dot_general(cg, bg, (((1,), (1,)), ((), ())), preferred_element_type=F32)
        for hh in range(hg):
            h = g * hg + hh
            xh = xs_ref[:, h * p:(h + 1) * p]
            lm = jnp.exp(jnp.where(mask, cs[:, h:h + 1] - cs_t[h:h + 1, :], -jnp.inf))
            m = (cb * lm * dt_t[h:h + 1, :]).astype(BF16)
            yd = _dot(m, xh.astype(BF16))
            sp = st_ref[h]
            yo = _dot(cg, sp.astype(BF16)) * ecs[:, h:h + 1]
            y_ref[:, h * p:(h + 1) * p] = yd + yo
            xw = (xh * dst[:, h:h + 1]).astype(BF16)
            stn = lax.dot_general(bg, xw, (((0,), (0,)), ((), ())), preferred_element_type=F32)
            st_ref[h] = sp * etot[:, h:h + 1] + stn


def _ssd_dir(xs, bc, dt, bias, a, bsz, seqlen, heads, reverse):
    t, d_inner = xs.shape
    n = SSD_CHUNK
    nc = seqlen // n

    def blk(b, j):
        return (b * nc + ((nc - 1 - j) if reverse else j), 0)

    return pl.pallas_call(
        functools.partial(_ssd_body, reverse=reverse, heads=heads),
        grid=(bsz, nc),
        in_specs=[pl.BlockSpec((n, d_inner), blk), pl.BlockSpec((n, bc.shape[1]), blk),
                  pl.BlockSpec((n, LANES), blk), _full(bias.shape), _full(a.shape)],
        out_specs=pl.BlockSpec((n, d_inner), blk),
        out_shape=jax.ShapeDtypeStruct((t, d_inner), F32),
        scratch_shapes=[pltpu.VMEM((heads, SSD_STATE, SSD_HEADDIM), F32)],
        compiler_params=_params(("parallel", "arbitrary")),
        name="ssd_bwd" if reverse else "ssd_fwd",
    )(xs, bc, dt, bias, a)


def _gelu(x):
    return 0.5 * x * (1.0 + lax.erf(x * (2.0 ** -0.5)))


def _merge_body(x_ref, u_ref, y5f_ref, y5b_ref, d5_ref, wglu_ref, ga_ref,
                xs_ref, ysf_ref, ysb_ref, dss_ref, z_ref, nw_ref, wso_ref, gb_ref, wo_ref, fnw_ref,
                h_ref, xn_ref, *, d_model, groups):
    ya = u_ref[...].astype(F32) * d5_ref[...] + y5f_ref[...] + y5b_ref[...]
    glu = _dot(_gelu(ya).astype(BF16), wglu_ref[...])
    ya = glu[:, :d_model] * jax.nn.sigmoid(glu[:, d_model:])
    z = z_ref[...]
    yb = (xs_ref[...] * dss_ref[...] + ysf_ref[...] + ysb_ref[...]) * (z * jax.nn.sigmoid(z))
    gw = yb.shape[1] // groups
    parts = []
    for g in range(groups):
        yg = yb[:, g * gw:(g + 1) * gw]
        parts.append(yg * lax.rsqrt(jnp.mean(yg * yg, axis=-1, keepdims=True) + RMS_EPS))
    yb = jnp.concatenate(parts, axis=1) * nw_ref[...]
    yb = _dot(yb.astype(BF16), wso_ref[...])
    merged = jax.nn.sigmoid(ga_ref[...]) * ya + jax.nn.sigmoid(gb_ref[...]) * yb
    h = x_ref[...] + _dot(merged.astype(BF16), wo_ref[...])
    h_ref[...] = h
    xn_ref[...] = _rms(h, fnw_ref[...])


def _merge(x2, u, y5f, y5b, d5, wglu, ga, xs, ysf, ysb, dss, z, nw, wso, gb, wo, fnw, tm=256):
    t, d = x2.shape
    args = [x2, u, y5f, y5b, d5, wglu, ga, xs, ysf, ysb, dss, z, nw, wso, gb, wo, fnw]
    per_token = [True, True, True, True, False, False, True, True, True, True, False, True, False, False, True,
                 False, False]
    specs = []
    for a, tok in zip(args, per_token):
        if tok:
            specs.append(pl.BlockSpec((tm, a.shape[1]), lambda i: (i, 0)))
        else:
            specs.append(_full(a.shape))
    return pl.pallas_call(
        functools.partial(_merge_body, d_model=d, groups=SSD_GROUPS),
        grid=(t // tm,),
        in_specs=specs,
        out_specs=[pl.BlockSpec((tm, d), lambda i: (i, 0))] * 2,
        out_shape=[jax.ShapeDtypeStruct((t, d), F32)] * 2,
        compiler_params=_params(("parallel",)),
        name="merge",
    )(*args)


def _make_mixer(mix_norm_w, w_in, s5_a_re, s5_a_im, s5_log_dt, s5_b_re, s5_b_im, s5_c_re, s5_c_im,
                s5_d, w_glu, conv_w, conv_b, ssd_dt_bias, ssd_a_log, ssd_d, ssd_norm_w, w_ssd_out, w_o, ffn_norm_w):
    d = w_in.shape[0]
    s5w = s5_d.shape[0]
    d_inner = w_ssd_out.shape[0]
    heads = ssd_d.shape[0]
    cdim = conv_b.shape[0]
    o1 = s5w
    o2 = o1 + d_inner
    o3 = o2 + cdim
    o4 = o3 + 2 * heads

    def padw(w):
        return jnp.pad(w, ((0, 0), (0, LANES - w.shape[1])))

    ws = [w_in[:, :o1], w_in[:, o1:o2], w_in[:, o2:o3], padw(w_in[:, o3:o3 + heads]),
          padw(w_in[:, o3 + heads:o4]), w_in[:, o4:o4 + d], w_in[:, o4 + d:]]
    ws = [w.astype(BF16) for w in ws]
    nw = mix_norm_w.reshape(1, d).astype(F32)
    s5p = [_s5_prep(s5_a_re[k], s5_a_im[k], s5_log_dt[k], s5_b_re[k], s5_b_im[k], s5_c_re[k], s5_c_im[k],
                    bool(k)) for k in (0, 1)]
    cw = conv_w.reshape(conv_w.shape[0], cdim).astype(F32)
    cb = conv_b.reshape(1, cdim).astype(F32)
    bias = [jnp.pad(ssd_dt_bias[k].astype(F32), (0, LANES - heads)).reshape(1, LANES) for k in (0, 1)]
    a_neg = [jnp.pad(-jnp.exp(ssd_a_log[k].astype(F32)), (0, LANES - heads)).reshape(1, LANES) for k in (0, 1)]
    dss = jnp.repeat(ssd_d.astype(F32), d_inner // heads).reshape(1, d_inner)
    d5 = s5_d.reshape(1, s5w).astype(F32)
    wglu = w_glu.astype(BF16)
    snw = ssd_norm_w.reshape(1, d_inner).astype(F32)
    wso = w_ssd_out.astype(BF16)
    wo = w_o.astype(BF16)
    fnw = ffn_norm_w.reshape(1, d).astype(F32)

    def run(x2, bsz, seqlen):
        u, z, xbc, dtf, dtb, ga, gb = _in_proj(x2, nw, ws)
        y5 = [_s5_dir(u, *s5p[k], bsz, seqlen, bool(k)) for k in (0, 1)]
        xs, bc = _conv(xbc, cw, cb, seqlen, d_inner)
        ys = [_ssd_dir(xs, bc, (dtf, dtb)[k], bias[k], a_neg[k], bsz, seqlen, heads, bool(k)) for k in (0, 1)]
        return _merge(x2, u, y5[0], y5[1], d5, wglu, ga, xs, ys[0], ys[1], dss, z, snw, wso, gb, wo, fnw)

    return run


def _split_bf16(x):
    hi = x.astype(BF16)
    return hi, (x - hi.astype(F32)).astype(BF16)


def _dot3(ah, al, bh, bl, dims=(((1,), (0,)), ((), ()))):
    def dg(p, q):
        return lax.dot_general(p, q, dims, preferred_element_type=F32)
    return dg(ah, bh) + (dg(al, bh) + dg(ah, bl))


def _topk_rows(xs, k, payloads=None):
    n = xs[0].shape[0]
    ri = lax.broadcasted_iota(jnp.int32, xs[0].shape, 0)
    xs = list(xs)
    vals = [[] for _ in xs]
    outs = [[] for _ in xs]
    for _ in range(k):
        for i, x in enumerate(xs):
            m = jnp.max(x, axis=0, keepdims=True)
            pos = jnp.min(jnp.where(x == m, ri, n), axis=0, keepdims=True)
            sel = ri == pos
            vals[i].append(m)
            if payloads is None:
                outs[i].append(pos)
            else:
                outs[i].append(jnp.max(jnp.where(sel, payloads[i], -1), axis=0, keepdims=True))
            xs[i] = jnp.where(sel, -jnp.inf, x)
    return [(jnp.concatenate(v, axis=0), jnp.concatenate(o, axis=0)) for v, o in zip(vals, outs)]


def _pair_candidates(c1, c2, combine):
    r = SUBLANES
    top = lax.broadcasted_iota(jnp.int32, (r, c1.shape[1]), 0) < r // 2
    lo = c2[0:r]
    quad = jnp.concatenate([c2[0:r // 2], c2[0:r // 2]], axis=0)
    blocks = [combine(c1[0:1], lo), combine(c1[0:1], c2[r:2 * r])]
    blocks += [combine(c1[a:a + 1], lo) for a in (1, 2, 3)]
    blocks += [combine(jnp.where(top, c1[a:a + 1], c1[a + 1:a + 2]), quad) for a in (4, 6)]
    blocks.append(combine(c1[r:2 * r], c2[0:1]))
    return jnp.concatenate(blocks, axis=0)


def _route_body(xn_ref, wqh_ref, wql_ref, kh_ref, kl_ref, e_ref, g_ref, q_ref, *, heads, nk, dk):
    k = PEER_TOPK
    xh, xl = _split_bf16(xn_ref[...])
    q_ref[...] = _dot3(xh, xl, wqh_ref[...], wql_ref[...])
    nt = (((1,), (1,)), ((), ()))

    def head(h, _):
        scores = []
        for s in range(2):
            col = pl.multiple_of((h * 2 + s) * dk, dk)
            qh, ql = _split_bf16(q_ref[:, pl.ds(col, dk)])
            scores.append(_dot3(kh_ref[h * 2 + s], kl_ref[h * 2 + s], qh, ql, nt))
        (s1, i1), (s2, i2) = _topk_rows(scores, k)
        cand = _pair_candidates(s1, s2, lambda a, b: a + b)
        ids = _pair_candidates(i1, i2, lambda a, b: a * nk + b)
        (best, expert), = _topk_rows([cand], k, [ids])
        w = jnp.exp(best - best[0:1, :])
        gate = w / jnp.sum(w, axis=0, keepdims=True)
        rows = pl.ds(pl.multiple_of(h * k, k), k)
        e_ref[0, rows, :] = expert
        g_ref[0, rows, :] = gate
        return 0

    lax.fori_loop(0, heads, head, 0)


def _peer_route(xn, wqh, wql, kh, kl, heads, nk, tm=LANES):
    t, d = xn.shape
    dk = kh.shape[-1]
    nb = t // tm
    slots = heads * PEER_TOPK
    return pl.pallas_call(
        functools.partial(_route_body, heads=heads, nk=nk, dk=dk),
        grid=(nb,),
        in_specs=[pl.BlockSpec((tm, d), lambda i: (i, 0)), _full(wqh.shape), _full(wql.shape),
                  _full(kh.shape), _full(kl.shape)],
        out_specs=[pl.BlockSpec((1, slots, tm), lambda i: (i, 0, 0))] * 2,
        out_shape=[jax.ShapeDtypeStruct((nb, slots, tm), jnp.int32),
                   jax.ShapeDtypeStruct((nb, slots, tm), F32)],
        scratch_shapes=[pltpu.VMEM((tm, wqh.shape[1]), F32)],
        compiler_params=_params(("parallel",)),
        name="peer_route",
    )(xn, wqh, wql, kh, kl)


SC_CORES = 2
SC_SUBCORES = 16
GATHER_WINDOW = 64


def _gather_rows(table, idx):
    from jax.experimental.pallas import tpu_sc as plsc
    n = idx.shape[0]
    dm = table.shape[1]
    win = GATHER_WINDOW
    workers = SC_CORES * SC_SUBCORES
    per = n // workers
    nwin = per // win
    idx3 = idx.reshape(workers, nwin, win)
    mesh = plsc.VectorSubcoreMesh(core_axis_name="c", subcore_axis_name="s")

    @functools.partial(
        pl.kernel, mesh=mesh,
        out_type=jax.ShapeDtypeStruct((n, dm), table.dtype),
        scratch_types=[pltpu.VMEM((nwin, win), jnp.int32),
                       pltpu.VMEM((win, dm), table.dtype), pltpu.VMEM((win, dm), table.dtype),
                       pltpu.SemaphoreType.DMA, pltpu.SemaphoreType.DMA,
                       pltpu.SemaphoreType.DMA, pltpu.SemaphoreType.DMA],
        name="peer_gather",
    )
    def gather(table_hbm, idx_hbm, out_hbm, idx_v, rows_a, rows_b, ga, gb, wa, wb):
        wid = lax.axis_index("s") * SC_CORES + lax.axis_index("c")
        pltpu.sync_copy(idx_hbm.at[wid], idx_v)

        @pl.loop(0, nwin, step=2)
        def _(w):
            in_a = pltpu.async_copy(table_hbm.at[idx_v.at[w]], rows_a, ga)
            in_b = pltpu.async_copy(table_hbm.at[idx_v.at[w + 1]], rows_b, gb)
            base = wid * per + w * win
            in_a.wait()
            out_a = pltpu.async_copy(rows_a, out_hbm.at[pl.ds(base, win)], wa)
            in_b.wait()
            out_b = pltpu.async_copy(rows_b, out_hbm.at[pl.ds(base + win, win)], wb)
            out_a.wait()
            out_b.wait()

    return gather(table, idx3)


SC_LANES = 16
HID_UNIT = 32


def _sc_hidden(u_pack, idx_tok, xn):
    from jax.experimental.pallas import tpu_sc as plsc
    t, slots = idx_tok.shape
    half = u_pack.shape[1]
    d = xn.shape[1]
    workers = SC_CORES * SC_SUBCORES
    tpw = t // workers
    upt = slots // HID_UNIT
    nl = SC_LANES
    idx3 = idx_tok.reshape(workers, tpw * upt, HID_UNIT)
    mesh = plsc.VectorSubcoreMesh(core_axis_name="c", subcore_axis_name="s")

    @functools.partial(
        pl.kernel, mesh=mesh,
        out_type=jax.ShapeDtypeStruct((t, slots), F32),
        scratch_types=[pltpu.VMEM((tpw * upt, HID_UNIT), jnp.int32),
                       pltpu.VMEM((HID_UNIT, half), jnp.uint32), pltpu.VMEM((HID_UNIT, half), jnp.uint32),
                       pltpu.VMEM((d,), F32), pltpu.VMEM((d,), F32),
                       pltpu.VMEM((tpw, slots), F32),
                       pltpu.SemaphoreType.DMA, pltpu.SemaphoreType.DMA,
                       pltpu.SemaphoreType.DMA, pltpu.SemaphoreType.DMA],
        compiler_params=pltpu.CompilerParams(needs_layout_passes=False),
        name="peer_hidden",
    )
    def hidden(u_hbm, idx_hbm, x_hbm, hid_hbm, idx_v, rows_a, rows_b, x_a, x_b, hid_v, ga, gb, xa, xb):
        wid = lax.axis_index("s") * SC_CORES + lax.axis_index("c")
        tok0 = wid * tpw
        pltpu.sync_copy(idx_hbm.at[wid], idx_v)
        bufs = ((rows_a, ga), (rows_b, gb))
        xbufs = ((x_a, xa), (x_b, xb))

        def rows_copy(unit, which):
            buf, sem = bufs[which]
            return pltpu.make_async_copy(u_hbm.at[idx_v.at[unit]], buf, sem)

        def x_copy(tok, which):
            buf, sem = xbufs[which]
            return pltpu.make_async_copy(x_hbm.at[tok0 + tok], buf, sem)

        def compute(rows_ref, x_ref, tok, q):
            lane = lax.iota(jnp.int32, nl)
            for grp in range(HID_UNIT // nl):
                def body(j, acc, grp=grp):
                    off = pl.multiple_of(j * nl, nl)
                    x_lo = x_ref[pl.ds(off, nl)]
                    x_hi = x_ref[pl.ds(half + off, nl)]
                    new = []
                    for r in range(nl):
                        w = rows_ref[grp * nl + r, pl.ds(off, nl)]
                        lo = lax.bitcast_convert_type(w << 16, F32)
                        hi = lax.bitcast_convert_type(w & jnp.uint32(0xFFFF0000), F32)
                        new.append(acc[r] + lo * x_lo + hi * x_hi)
                    return tuple(new)

                acc = lax.fori_loop(0, half // nl, body, tuple(jnp.zeros((nl,), F32) for _ in range(nl)))
                res = jnp.zeros((nl,), F32)
                for r in range(nl):
                    res = jnp.where(lane == r, jnp.sum(acc[r]), res)
                hid_v[tok, pl.ds(q * HID_UNIT + grp * nl, nl)] = res

        rows_copy(0, 0).start()
        x_copy(0, 0).start()

        @pl.loop(0, tpw, step=2)
        def _(t0):
            for tt in range(2):
                tok = t0 + tt
                x_copy(tok, tt).wait()

                @pl.when(tok + 1 < tpw)
                def _():
                    x_copy(tok + 1, 1 - tt).start()

                for q in range(upt):
                    unit = tok * upt + q
                    if q + 1 < upt:
                        rows_copy(unit + 1, (q + 1) % 2).start()
                    else:
                        @pl.when(tok + 1 < tpw)
                        def _():
                            rows_copy(unit + 1, (q + 1) % 2).start()
                    rows_copy(unit, q % 2).wait()
                    compute(bufs[q % 2][0], xbufs[tt][0], tok, q)

        pltpu.sync_copy(hid_v, hid_hbm.at[pl.ds(tok0, tpw)])

    return hidden(u_pack, idx3, xn)


EVAL_TILES = 2


def _pack_bf16_halves(tab):
    half = tab.shape[1] // 2
    bits = lax.bitcast_convert_type(tab.astype(BF16), jnp.uint16).astype(jnp.uint32)
    return bits[:, :half] | (bits[:, half:] << 16)


def _unpack_bf16_halves(w):
    lo = lax.bitcast_convert_type(w << 16, F32)
    hi = lax.bitcast_convert_type(w & jnp.uint32(0xFFFF0000), F32)
    return lo, hi


def _eval_body(vg_ref, hid_ref, g_ref, h_ref, fw_ref, o_ref, *, slots):
    r = SUBLANES
    half = h_ref.shape[1] // 2
    for tile in range(EVAL_TILES):
        rows = slice(tile * r, (tile + 1) * r)
        act = _gelu(hid_ref[rows, :]) * g_ref[rows, :]
        acc_lo = jnp.zeros((r, half), F32)
        acc_hi = jnp.zeros((r, half), F32)
        for k in range(slots):
            v_lo, v_hi = _unpack_bf16_halves(vg_ref[tile, k * r:(k + 1) * r, :])
            a = act[:, k:k + 1]
            acc_lo = acc_lo + a * v_lo
            acc_hi = acc_hi + a * v_hi
        o_ref[rows, :] = _rms(h_ref[rows, :] + jnp.concatenate([acc_lo, acc_hi], axis=1), fw_ref[...])


def _peer_eval(vg, hid, gates, h, fw, slots):
    t, d = h.shape
    r = SUBLANES
    g = EVAL_TILES
    vg = vg.reshape(t // r, slots * r, d // 2)
    return pl.pallas_call(
        functools.partial(_eval_body, slots=slots),
        grid=(t // (r * g),),
        in_specs=[pl.BlockSpec((g, slots * r, d // 2), lambda i: (i, 0, 0)),
                  pl.BlockSpec((g * r, slots), lambda i: (i, 0)), pl.BlockSpec((g * r, slots), lambda i: (i, 0)),
                  pl.BlockSpec((g * r, d), lambda i: (i, 0)), _full(fw.shape)],
        out_specs=pl.BlockSpec((g * r, d), lambda i: (i, 0)),
        out_shape=jax.ShapeDtypeStruct((t, d), F32),
        compiler_params=_params(("parallel",)),
        name="peer_eval",
    )(vg, hid, gates, h, fw)


PEER_CHUNK_TOKENS = 2048


def _make_peer(w_q, sub_keys, u_tab, v_tab, final_norm_w):
    d = w_q.shape[0]
    heads, _, nk, dk = sub_keys.shape
    slots = heads * PEER_TOPK
    wqh, wql = _split_bf16(w_q.astype(F32))
    kh, kl = _split_bf16(sub_keys.reshape(heads * 2, nk, dk).astype(F32))
    fw = final_norm_w.reshape(1, d).astype(F32)
    u_pack = _pack_bf16_halves(u_tab.astype(F32))
    v_pack = _pack_bf16_halves(v_tab.astype(F32))
    r = SUBLANES

    def route(xn):
        t = xn.shape[0]
        expert_t, gates_t = _peer_route(xn, wqh, wql, kh, kl, heads, nk)
        nb, _, tm = expert_t.shape
        idx = expert_t.reshape(nb, slots, tm // r, r).transpose(0, 2, 1, 3).reshape(t // r, slots * r)
        idx_tok = expert_t.transpose(0, 2, 1).reshape(t, slots)
        gates = gates_t.transpose(0, 2, 1).reshape(t, slots)
        return idx, idx_tok, gates

    def experts(routed, xn, h):
        idx, idx_tok, gates = routed
        t = xn.shape[0]
        tc = min(PEER_CHUNK_TOKENS, t)
        outs = []
        for c in range(t // tc):
            rows = slice(c * tc, (c + 1) * tc)
            hid = _sc_hidden(u_pack, idx_tok[rows], xn[rows])
            vg = _gather_rows(v_pack, idx[c * tc // r:(c + 1) * tc // r].reshape(-1))
            outs.append(_peer_eval(vg, hid, gates[rows], h[rows], fw, slots))
        return outs

    return route, experts


def kernel(x, mix_norm_w, w_in, s5_a_re, s5_a_im, s5_log_dt, s5_b_re, s5_b_im, s5_c_re, s5_c_im, s5_d, w_glu, conv_w, conv_b, ssd_dt_bias, ssd_a_log, ssd_d, ssd_norm_w, w_ssd_out, w_o, ffn_norm_w, peer_w_q, peer_sub_keys, peer_u, peer_v, final_norm_w):
    bsz, seqlen, d = x.shape
    mixer = _make_mixer(mix_norm_w[0], w_in[0], s5_a_re[0], s5_a_im[0], s5_log_dt[0], s5_b_re[0], s5_b_im[0],
                        s5_c_re[0], s5_c_im[0], s5_d[0], w_glu[0], conv_w[0], conv_b[0], ssd_dt_bias[0],
                        ssd_a_log[0], ssd_d[0], ssd_norm_w[0], w_ssd_out[0], w_o[0], ffn_norm_w[0])
    route, experts = _make_peer(peer_w_q[0], peer_sub_keys[0], peer_u[0], peer_v[0], final_norm_w)
    outs = []
    for b in range(bsz):
        h, xn = mixer(x[b], 1, seqlen)
        outs += experts(route(xn), xn, h)
    return jnp.concatenate(outs, axis=0).reshape(bsz, seqlen, d).astype(x.dtype)
```

```python
import functools

import jax
import jax.numpy as jnp
from jax import lax
from jax.experimental import pallas as pl
from jax.experimental.pallas import tpu as pltpu

F32 = jnp.float32
BF16 = jnp.bfloat16

RMS_EPS = 1e-6
SUBLANES = 8
LANES = 128
SSD_CHUNK = 128
SSD_HEADDIM = 64
SSD_GROUPS = 4
SSD_STATE = 64
PEER_TOPK = 16
VMEM_LIMIT = 56 * 1024 * 1024


def _params(sem):
    return pltpu.CompilerParams(dimension_semantics=sem, vmem_limit_bytes=VMEM_LIMIT)


def _rms(x, w):
    return x * lax.rsqrt(jnp.mean(x * x, axis=-1, keepdims=True) + RMS_EPS) * w


def _dot(a, b):
    return jnp.dot(a, b, preferred_element_type=F32)


def _full(shape):
    n = len(shape)
    return pl.BlockSpec(shape, lambda *_: (0,) * n)


def _inproj_body(x_ref, nw_ref, wu_ref, wz_ref, wx_ref, wdf_ref, wdb_ref, wga_ref, wgb_ref,
                 u_ref, z_ref, xbc_ref, dtf_ref, dtb_ref, ga_ref, gb_ref):
    xb = _rms(x_ref[...], nw_ref[...]).astype(BF16)
    u_ref[...] = _dot(xb, wu_ref[...]).astype(BF16)
    z_ref[...] = _dot(xb, wz_ref[...])
    xbc_ref[...] = _dot(xb, wx_ref[...])
    dtf_ref[...] = _dot(xb, wdf_ref[...])
    dtb_ref[...] = _dot(xb, wdb_ref[...])
    ga_ref[...] = _dot(xb, wga_ref[...])
    gb_ref[...] = _dot(xb, wgb_ref[...])


def _in_proj(x2, nw, ws, tm=512):
    t, d = x2.shape
    widths = [w.shape[1] for w in ws]
    dts = [BF16] + [F32] * 6
    return pl.pallas_call(
        _inproj_body,
        grid=(t // tm,),
        in_specs=[pl.BlockSpec((tm, d), lambda i: (i, 0)), _full((1, d))] + [_full(w.shape) for w in ws],
        out_specs=[pl.BlockSpec((tm, n), lambda i: (i, 0)) for n in widths],
        out_shape=[jax.ShapeDtypeStruct((t, n), dt) for n, dt in zip(widths, dts)],
        compiler_params=_params(("parallel",)),
        name="in_proj",
    )(x2, nw, *ws)


S5_LANE_GROUP = 4


def _s5_body(u_ref, bd_ref, cd_ref, tab_ref, y_ref, s_ref, car_ref, *, reverse, lb, ns):
    @pl.when(pl.program_id(1) == 0)
    def _():
        car_ref[...] = jnp.zeros_like(car_ref)

    s_ref[...] = _dot(u_ref[...], bd_ref[...])
    nt = lb // SUBLANES
    edge = 0 if reverse else SUBLANES - 1

    for c0 in range(0, ns // LANES, S5_LANE_GROUP):
        los = [(c0 + g) * LANES for g in range(S5_LANE_GROUP)]

        def tile_step(i, carry, los=los):
            r = (nt - 1 - i) if reverse else i
            rows = pl.ds(pl.multiple_of(r * SUBLANES, SUBLANES), SUBLANES)
            new = []
            for g, lo in enumerate(los):
                re = s_ref[rows, lo:lo + LANES]
                im = s_ref[rows, ns + lo:ns + lo + LANES]
                for kk, k in enumerate((1, 2, 4)):
                    pr = tab_ref[2 * kk, :, lo:lo + LANES]
                    pi = tab_ref[2 * kk + 1, :, lo:lo + LANES]
                    sh = (SUBLANES - k) if reverse else k
                    rr = pltpu.roll(re, sh, 0)
                    ri = pltpu.roll(im, sh, 0)
                    re, im = re + pr * rr - pi * ri, im + pr * ri + pi * rr
                pr = tab_ref[6, :, lo:lo + LANES]
                pi = tab_ref[7, :, lo:lo + LANES]
                cr, ci = carry[2 * g], carry[2 * g + 1]
                re, im = re + pr * cr - pi * ci, im + pr * ci + pi * cr
                s_ref[rows, lo:lo + LANES] = re
                s_ref[rows, ns + lo:ns + lo + LANES] = im
                new.append(jnp.broadcast_to(re[edge:edge + 1, :], (SUBLANES, LANES)))
                new.append(jnp.broadcast_to(im[edge:edge + 1, :], (SUBLANES, LANES)))
            return tuple(new)

        carry0 = []
        for lo in los:
            carry0.append(car_ref[0, :, lo:lo + LANES])
            carry0.append(car_ref[1, :, lo:lo + LANES])
        carry = lax.fori_loop(0, nt, tile_step, tuple(carry0))
        for g, lo in enumerate(los):
            car_ref[0, :, lo:lo + LANES] = carry[2 * g]
            car_ref[1, :, lo:lo + LANES] = carry[2 * g + 1]

    y_ref[...] = _dot(s_ref[...].astype(BF16), cd_ref[...])


def _s5_dir(u, bd, cd, tab, bsz, seqlen, reverse, lb=256):
    t, w = u.shape
    ns = bd.shape[1] // 2
    nb = seqlen // lb

    def blk(b, j):
        return (b * nb + ((nb - 1 - j) if reverse else j), 0)

    return pl.pallas_call(
        functools.partial(_s5_body, reverse=reverse, lb=lb, ns=ns),
        grid=(bsz, nb),
        in_specs=[pl.BlockSpec((lb, w), blk), _full(bd.shape), _full(cd.shape), _full(tab.shape)],
        out_specs=pl.BlockSpec((lb, w), blk),
        out_shape=jax.ShapeDtypeStruct((t, w), F32),
        scratch_shapes=[pltpu.VMEM((lb, 2 * ns), F32), pltpu.VMEM((2, SUBLANES, ns), F32)],
        compiler_params=_params(("parallel", "arbitrary")),
        name="s5_bwd" if reverse else "s5_fwd",
    )(u, bd, cd, tab)


def _s5_prep(a_re, a_im, log_dt, b_re, b_im, c_re, c_im, reverse):
    g, p = a_re.shape
    c = b_re.shape[-1]
    ar = a_re.astype(F32)
    ai = a_im.astype(F32)
    step = jnp.exp(log_dt.astype(F32))[:, None]
    mag = jnp.exp(ar * step)
    lam_r = mag * jnp.cos(ai * step)
    lam_i = mag * jnp.sin(ai * step)
    den = ar * ar + ai * ai
    f_r = ((lam_r - 1.0) * ar + lam_i * ai) / den
    f_i = (lam_i * ar - (lam_r - 1.0) * ai) / den
    bt_r = f_r[..., None] * b_re - f_i[..., None] * b_im
    bt_i = f_r[..., None] * b_im + f_i[..., None] * b_re
    eye = jnp.eye(g, dtype=F32)
    bd = jnp.concatenate([jnp.einsum('gpc,gh->gchp', bt_r, eye).reshape(g * c, g * p),
                          jnp.einsum('gpc,gh->gchp', bt_i, eye).reshape(g * c, g * p)], axis=1)
    cd = jnp.concatenate([jnp.einsum('gcp,gh->gphc', c_re, eye).reshape(g * p, g * c),
                          -jnp.einsum('gcp,gh->gphc', c_im, eye).reshape(g * p, g * c)], axis=0)
    lr = lam_r.reshape(1, g * p)
    li = lam_i.reshape(1, g * p)
    pw = [(lr, li)]
    for _ in range(SUBLANES - 1):
        qr, qi = pw[-1]
        pw.append((qr * lr - qi * li, qr * li + qi * lr))
    row = jnp.arange(SUBLANES)[:, None]
    tabs = []
    for k in (1, 2, 4):
        keep = (row < SUBLANES - k) if reverse else (row >= k)
        for comp in pw[k - 1]:
            tabs.append(jnp.where(keep, comp, 0.0))
    order = list(range(SUBLANES - 1, -1, -1)) if reverse else list(range(SUBLANES))
    for comp in (0, 1):
        tabs.append(jnp.concatenate([pw[k][comp] for k in order], axis=0))
    return bd.astype(BF16), cd.astype(BF16), jnp.stack(tabs, axis=0)


def _conv_body(cur_ref, prev_ref, nxt_ref, w_ref, b_ref, xs_ref, bc_ref, *, nblk, width, d_inner):
    j = pl.program_id(0) % nblk
    pad = width // 2
    cur = cur_ref[...]
    prev = jnp.where(j == 0, 0.0, prev_ref[...])
    nxt = jnp.where(j == nblk - 1, 0.0, nxt_ref[...])
    ext = jnp.concatenate([prev, cur, nxt], axis=0)
    tm = cur.shape[0]
    acc = jnp.zeros_like(cur) + b_ref[...]
    for k in range(width):
        off = SUBLANES - pad + k
        acc = acc + ext[off:off + tm, :] * w_ref[k:k + 1, :]
    act = acc * jax.nn.sigmoid(acc)
    xs_ref[...] = act[:, :d_inner]
    bc_ref[...] = act[:, d_inner:].astype(BF16)


def _conv(xbc, w, b, seqlen, d_inner, tm=256):
    t, cdim = xbc.shape
    width = w.shape[0]
    nblk = seqlen // tm
    r = tm // SUBLANES
    last = t // SUBLANES - 1
    return pl.pallas_call(
        functools.partial(_conv_body, nblk=nblk, width=width, d_inner=d_inner),
        grid=(t // tm,),
        in_specs=[pl.BlockSpec((tm, cdim), lambda i: (i, 0)),
                  pl.BlockSpec((SUBLANES, cdim), lambda i: (jnp.maximum(i * r - 1, 0), 0)),
                  pl.BlockSpec((SUBLANES, cdim), lambda i: (jnp.minimum((i + 1) * r, last), 0)),
                  _full(w.shape), _full(b.shape)],
        out_specs=[pl.BlockSpec((tm, d_inner), lambda i: (i, 0)),
                   pl.BlockSpec((tm, cdim - d_inner), lambda i: (i, 0))],
        out_shape=[jax.ShapeDtypeStruct((t, d_inner), F32),
                   jax.ShapeDtypeStruct((t, cdim - d_inner), BF16)],
        compiler_params=_params(("parallel",)),
        name="ssd_conv",
    )(xbc, xbc, xbc, w, b)


def _cumsum_rows(x):
    n = x.shape[0]
    ri = lax.broadcasted_iota(jnp.int32, x.shape, 0)
    k = 1
    while k < n:
        x = x + jnp.where(ri >= k, pltpu.roll(x, k, 0), 0.0)
        k *= 2
    return x


def _softplus(x):
    return jnp.maximum(x, 0.0) + jnp.log1p(jnp.exp(-jnp.abs(x)))


def _ssd_body(xs_ref, bc_ref, dt_ref, bias_ref, a_ref, y_ref, st_ref, *, reverse, heads):
    @pl.when(pl.program_id(1) == 0)
    def _():
        st_ref[...] = jnp.zeros_like(st_ref)

    n = SSD_CHUNK
    p = SSD_HEADDIM
    ng = SSD_GROUPS
    ns = SSD_STATE
    hg = heads // ng
    dt = _softplus(dt_ref[...] + bias_ref[...])
    adt = dt * a_ref[...]
    cs = _cumsum_rows(adt)
    tot = cs[n - 1:n, :]
    if reverse:
        cs = tot - cs + adt
    cs_t = cs.T
    dt_t = dt.T
    ecs = jnp.exp(cs)
    dst = jnp.exp(tot - cs) * dt
    etot = jnp.exp(tot)
    li = lax.broadcasted_iota(jnp.int32, (n, n), 0)
    si = lax.broadcasted_iota(jnp.int32, (n, n), 1)
    mask = (si >= li) if reverse else (si <= li)
    for g in range(ng):
        bg = bc_ref[:, g * ns:(g + 1) * ns]
        cg = bc_ref[:, ng * ns + g * ns:ng * ns + (g + 1) * ns]
        cb = lax.dot_general(cg, bg, (((1,), (1,)), ((), ())), preferred_element_type=F32)
        for hh in range(hg):
            h = g * hg + hh
            xh = xs_ref[:, h * p:(h + 1) * p]
            lm = jnp.exp(jnp.where(mask, cs[:, h:h + 1] - cs_t[h:h + 1, :], -jnp.inf))
            m = (cb * lm * dt_t[h:h + 1, :]).astype(BF16)
            yd = _dot(m, xh.astype(BF16))
            sp = st_ref[h]
            yo = _dot(cg, sp.astype(BF16)) * ecs[:, h:h + 1]
            y_ref[:, h * p:(h + 1) * p] = yd + yo
            xw = (xh * dst[:, h:h + 1]).astype(BF16)
            stn = lax.dot_general(bg, xw, (((0,), (0,)), ((), ())), preferred_element_type=F32)
            st_ref[h] = sp * etot[:, h:h + 1] + stn


def _ssd_dir(xs, bc, dt, bias, a, bsz, seqlen, heads, reverse):
    t, d_inner = xs.shape
    n = SSD_CHUNK
    nc = seqlen // n

    def blk(b, j):
        return (b * nc + ((nc - 1 - j) if reverse else j), 0)

    return pl.pallas_call(
        functools.partial(_ssd_body, reverse=reverse, heads=heads),
        grid=(bsz, nc),
        in_specs=[pl.BlockSpec((n, d_inner), blk), pl.BlockSpec((n, bc.shape[1]), blk),
                  pl.BlockSpec((n, LANES), blk), _full(bias.shape), _full(a.shape)],
        out_specs=pl.BlockSpec((n, d_inner), blk),
        out_shape=jax.ShapeDtypeStruct((t, d_inner), F32),
        scratch_shapes=[pltpu.VMEM((heads, SSD_STATE, SSD_HEADDIM), F32)],
        compiler_params=_params(("parallel", "arbitrary")),
        name="ssd_bwd" if reverse else "ssd_fwd",
    )(xs, bc, dt, bias, a)


def _gelu(x):
    return 0.5 * x * (1.0 + lax.erf(x * (2.0 ** -0.5)))


def _merge_body(x_ref, u_ref, y5f_ref, y5b_ref, d5_ref, wglu_ref, ga_ref,
                xs_ref, ysf_ref, ysb_ref, dss_ref, z_ref, nw_ref, wso_ref, gb_ref, wo_ref, fnw_ref,
                h_ref, xn_ref, *, d_model, groups):
    ya = u_ref[...].astype(F32) * d5_ref[...] + y5f_ref[...] + y5b_ref[...]
    glu = _dot(_gelu(ya).astype(BF16), wglu_ref[...])
    ya = glu[:, :d_model] * jax.nn.sigmoid(glu[:, d_model:])
    z = z_ref[...]
    yb = (xs_ref[...] * dss_ref[...] + ysf_ref[...] + ysb_ref[...]) * (z * jax.nn.sigmoid(z))
    gw = yb.shape[1] // groups
    parts = []
    for g in range(groups):
        yg = yb[:, g * gw:(g + 1) * gw]
        parts.append(yg * lax.rsqrt(jnp.mean(yg * yg, axis=-1, keepdims=True) + RMS_EPS))
    yb = jnp.concatenate(parts, axis=1) * nw_ref[...]
    yb = _dot(yb.astype(BF16), wso_ref[...])
    merged = jax.nn.sigmoid(ga_ref[...]) * ya + jax.nn.sigmoid(gb_ref[...]) * yb
    h = x_ref[...] + _dot(merged.astype(BF16), wo_ref[...])
    h_ref[...] = h
    xn_ref[...] = _rms(h, fnw_ref[...])


def _merge(x2, u, y5f, y5b, d5, wglu, ga, xs, ysf, ysb, dss, z, nw, wso, gb, wo, fnw, tm=256):
    t, d = x2.shape
    args = [x2, u, y5f, y5b, d5, wglu, ga, xs, ysf, ysb, dss, z, nw, wso, gb, wo, fnw]
    per_token = [True, True, True, True, False, False, True, True, True, True, False, True, False, False, True,
                 False, False]
    specs = []
    for a, tok in zip(args, per_token):
        if tok:
            specs.append(pl.BlockSpec((tm, a.shape[1]), lambda i: (i, 0)))
        else:
            specs.append(_full(a.shape))
    return pl.pallas_call(
        functools.partial(_merge_body, d_model=d, groups=SSD_GROUPS),
        grid=(t // tm,),
        in_specs=specs,
        out_specs=[pl.BlockSpec((tm, d), lambda i: (i, 0))] * 2,
        out_shape=[jax.ShapeDtypeStruct((t, d), F32)] * 2,
        compiler_params=_params(("parallel",)),
        name="merge",
    )(*args)


def _make_mixer(mix_norm_w, w_in, s5_a_re, s5_a_im, s5_log_dt, s5_b_re, s5_b_im, s5_c_re, s5_c_im,
                s5_d, w_glu, conv_w, conv_b, ssd_dt_bias, ssd_a_log, ssd_d, ssd_norm_w, w_ssd_out, w_o, ffn_norm_w):
    d = w_in.shape[0]
    s5w = s5_d.shape[0]
    d_inner = w_ssd_out.shape[0]
    heads = ssd_d.shape[0]
    cdim = conv_b.shape[0]
    o1 = s5w
    o2 = o1 + d_inner
    o3 = o2 + cdim
    o4 = o3 + 2 * heads

    def padw(w):
        return jnp.pad(w, ((0, 0), (0, LANES - w.shape[1])))

    ws = [w_in[:, :o1], w_in[:, o1:o2], w_in[:, o2:o3], padw(w_in[:, o3:o3 + heads]),
          padw(w_in[:, o3 + heads:o4]), w_in[:, o4:o4 + d], w_in[:, o4 + d:]]
    ws = [w.astype(BF16) for w in ws]
    nw = mix_norm_w.reshape(1, d).astype(F32)
    s5p = [_s5_prep(s5_a_re[k], s5_a_im[k], s5_log_dt[k], s5_b_re[k], s5_b_im[k], s5_c_re[k], s5_c_im[k],
                    bool(k)) for k in (0, 1)]
    cw = conv_w.reshape(conv_w.shape[0], cdim).astype(F32)
    cb = conv_b.reshape(1, cdim).astype(F32)
    bias = [jnp.pad(ssd_dt_bias[k].astype(F32), (0, LANES - heads)).reshape(1, LANES) for k in (0, 1)]
    a_neg = [jnp.pad(-jnp.exp(ssd_a_log[k].astype(F32)), (0, LANES - heads)).reshape(1, LANES) for k in (0, 1)]
    dss = jnp.repeat(ssd_d.astype(F32), d_inner // heads).reshape(1, d_inner)
    d5 = s5_d.reshape(1, s5w).astype(F32)
    wglu = w_glu.astype(BF16)
    snw = ssd_norm_w.reshape(1, d_inner).astype(F32)
    wso = w_ssd_out.astype(BF16)
    wo = w_o.astype(BF16)
    fnw = ffn_norm_w.reshape(1, d).astype(F32)

    def run(x2, bsz, seqlen):
        u, z, xbc, dtf, dtb, ga, gb = _in_proj(x2, nw, ws)
        y5 = [_s5_dir(u, *s5p[k], bsz, seqlen, bool(k)) for k in (0, 1)]
        xs, bc = _conv(xbc, cw, cb, seqlen, d_inner)
        ys = [_ssd_dir(xs, bc, (dtf, dtb)[k], bias[k], a_neg[k], bsz, seqlen, heads, bool(k)) for k in (0, 1)]
        return _merge(x2, u, y5[0], y5[1], d5, wglu, ga, xs, ys[0], ys[1], dss, z, snw, wso, gb, wo, fnw)

    return run


def _split_bf16(x):
    hi = x.astype(BF16)
    return hi, (x - hi.astype(F32)).astype(BF16)


def _dot3(ah, al, bh, bl, dims=(((1,), (0,)), ((), ()))):
    def dg(p, q):
        return lax.dot_general(p, q, dims, preferred_element_type=F32)
    return dg(ah, bh) + (dg(al, bh) + dg(ah, bl))


def _topk_rows(xs, k, payloads=None):
    n = xs[0].shape[0]
    ri = lax.broadcasted_iota(jnp.int32, xs[0].shape, 0)
    xs = list(xs)
    vals = [[] for _ in xs]
    outs = [[] for _ in xs]
    for _ in range(k):
        for i, x in enumerate(xs):
            m = jnp.max(x, axis=0, keepdims=True)
            pos = jnp.min(jnp.where(x == m, ri, n), axis=0, keepdims=True)
            sel = ri == pos
            vals[i].append(m)
            if payloads is None:
                outs[i].append(pos)
            else:
                outs[i].append(jnp.max(jnp.where(sel, payloads[i], -1), axis=0, keepdims=True))
            xs[i] = jnp.where(sel, -jnp.inf, x)
    return [(jnp.concatenate(v, axis=0), jnp.concatenate(o, axis=0)) for v, o in zip(vals, outs)]


def _pair_candidates(c1, c2, combine):
    r = SUBLANES
    top = lax.broadcasted_iota(jnp.int32, (r, c1.shape[1]), 0) < r // 2
    lo = c2[0:r]
    quad = jnp.concatenate([c2[0:r // 2], c2[0:r // 2]], axis=0)
    blocks = [combine(c1[0:1], lo), combine(c1[0:1], c2[r:2 * r])]
    blocks += [combine(c1[a:a + 1], lo) for a in (1, 2, 3)]
    blocks += [combine(jnp.where(top, c1[a:a + 1], c1[a + 1:a + 2]), quad) for a in (4, 6)]
    blocks.append(combine(c1[r:2 * r], c2[0:1]))
    return jnp.concatenate(blocks, axis=0)


def _route_body(xn_ref, wqh_ref, wql_ref, kh_ref, kl_ref, e_ref, g_ref, q_ref, *, heads, nk, dk):
    k = PEER_TOPK
    xh, xl = _split_bf16(xn_ref[...])
    q_ref[...] = _dot3(xh, xl, wqh_ref[...], wql_ref[...])
    nt = (((1,), (1,)), ((), ()))

    def head(h, _):
        scores = []
        for s in range(2):
            col = pl.multiple_of((h * 2 + s) * dk, dk)
            qh, ql = _split_bf16(q_ref[:, pl.ds(col, dk)])
            scores.append(_dot3(kh_ref[h * 2 + s], kl_ref[h * 2 + s], qh, ql, nt))
        (s1, i1), (s2, i2) = _topk_rows(scores, k)
        cand = _pair_candidates(s1, s2, lambda a, b: a + b)
        ids = _pair_candidates(i1, i2, lambda a, b: a * nk + b)
        (best, expert), = _topk_rows([cand], k, [ids])
        w = jnp.exp(best - best[0:1, :])
        gate = w / jnp.sum(w, axis=0, keepdims=True)
        rows = pl.ds(pl.multiple_of(h * k, k), k)
        e_ref[0, rows, :] = expert
        g_ref[0, rows, :] = gate
        return 0

    lax.fori_loop(0, heads, head, 0)


def _peer_route(xn, wqh, wql, kh, kl, heads, nk, tm=LANES):
    t, d = xn.shape
    dk = kh.shape[-1]
    nb = t // tm
    slots = heads * PEER_TOPK
    return pl.pallas_call(
        functools.partial(_route_body, heads=heads, nk=nk, dk=dk),
        grid=(nb,),
        in_specs=[pl.BlockSpec((tm, d), lambda i: (i, 0)), _full(wqh.shape), _full(wql.shape),
                  _full(kh.shape), _full(kl.shape)],
        out_specs=[pl.BlockSpec((1, slots, tm), lambda i: (i, 0, 0))] * 2,
        out_shape=[jax.ShapeDtypeStruct((nb, slots, tm), jnp.int32),
                   jax.ShapeDtypeStruct((nb, slots, tm), F32)],
        scratch_shapes=[pltpu.VMEM((tm, wqh.shape[1]), F32)],
        compiler_params=_params(("parallel",)),
        name="peer_route",
    )(xn, wqh, wql, kh, kl)


SC_CORES = 2
SC_SUBCORES = 16
SC_LANES = 16
ROW_UNIT = 32


def _sc_experts(uv_pack, idx, xn):
    from jax.experimental.pallas import tpu_sc as plsc
    r = SUBLANES
    n = idx.shape[0]
    t, d = xn.shape
    half = d // 2
    workers = SC_CORES * SC_SUBCORES
    per = n // workers
    tile_rows = n // (t // r)
    tiles = per // tile_rows
    upt = tile_rows // ROW_UNIT
    nl = SC_LANES
    idx3 = idx.reshape(workers, tiles * upt, ROW_UNIT)
    mesh = plsc.VectorSubcoreMesh(core_axis_name="c", subcore_axis_name="s")

    @functools.partial(
        pl.kernel, mesh=mesh,
        out_type=[jax.ShapeDtypeStruct((n,), F32), jax.ShapeDtypeStruct((n, half), jnp.uint32)],
        scratch_types=[pltpu.VMEM((tiles * upt, ROW_UNIT), jnp.int32),
                       pltpu.VMEM((ROW_UNIT, d), jnp.uint32), pltpu.VMEM((ROW_UNIT, d), jnp.uint32),
                       pltpu.VMEM((r, d), F32), pltpu.VMEM((r, d), F32),
                       pltpu.VMEM((per,), F32),
                       pltpu.SemaphoreType.DMA, pltpu.SemaphoreType.DMA,
                       pltpu.SemaphoreType.DMA, pltpu.SemaphoreType.DMA,
                       pltpu.SemaphoreType.DMA, pltpu.SemaphoreType.DMA],
        compiler_params=pltpu.CompilerParams(needs_layout_passes=False),
        name="peer_experts",
    )
    def experts(uv_hbm, idx_hbm, x_hbm, hid_hbm, vst_hbm, idx_v, rows_a, rows_b, x_a, x_b, hid_v,
                ga, gb, wa, wb, xa, xb):
        wid = lax.axis_index("s") * SC_CORES + lax.axis_index("c")
        row0 = wid * per
        tok0 = wid * tiles * r
        pltpu.sync_copy(idx_hbm.at[wid], idx_v)
        bufs = ((rows_a, ga, wa), (rows_b, gb, wb))
        xbufs = ((x_a, xa), (x_b, xb))

        def rows_copy(unit, which):
            buf, sem, _ = bufs[which]
            return pltpu.make_async_copy(uv_hbm.at[idx_v.at[unit]], buf, sem)

        def v_copy(unit, which):
            buf, _, sem = bufs[which]
            return pltpu.make_async_copy(buf.at[:, pl.ds(half, half)],
                                         vst_hbm.at[pl.ds(row0 + unit * ROW_UNIT, ROW_UNIT)], sem)

        def x_copy(tile, which):
            buf, sem = xbufs[which]
            return pltpu.make_async_copy(x_hbm.at[pl.ds(tok0 + tile * r, r)], buf, sem)

        def compute(rows_ref, x_ref, unit):
            lane = lax.iota(jnp.int32, nl)
            for grp in range(ROW_UNIT // nl):
                def body(j, acc, grp=grp):
                    off = pl.multiple_of(j * nl, nl)
                    x_lo = [x_ref[tk, pl.ds(off, nl)] for tk in range(r)]
                    x_hi = [x_ref[tk, pl.ds(half + off, nl)] for tk in range(r)]
                    new = []
                    for i in range(nl):
                        w = rows_ref[grp * nl + i, pl.ds(off, nl)]
                        lo = lax.bitcast_convert_type(w << 16, F32)
                        hi = lax.bitcast_convert_type(w & jnp.uint32(0xFFFF0000), F32)
                        new.append(acc[i] + lo * x_lo[i % r] + hi * x_hi[i % r])
                    return tuple(new)

                acc = lax.fori_loop(0, half // nl, body, tuple(jnp.zeros((nl,), F32) for _ in range(nl)))
                res = jnp.zeros((nl,), F32)
                for i in range(nl):
                    res = jnp.where(lane == i, jnp.sum(acc[i]), res)
                hid_v[pl.ds(unit * ROW_UNIT + grp * nl, nl)] = res

        rows_copy(0, 0).start()
        x_copy(0, 0).start()

        @pl.loop(0, tiles, step=2)
        def _(tile0):
            for tt in range(2):
                tile = tile0 + tt
                x_copy(tile, tt).wait()

                @pl.when(tile + 1 < tiles)
                def _():
                    x_copy(tile + 1, 1 - tt).start()

                @pl.loop(0, upt, step=2)
                def _(q0):
                    for qq in range(2):
                        unit = tile * upt + q0 + qq
                        @pl.when(unit > 0)
                        def _():
                            v_copy(unit - 1, 1 - qq).wait()

                        @pl.when(unit + 1 < tiles * upt)
                        def _():
                            rows_copy(unit + 1, 1 - qq).start()

                        rows_copy(unit, qq).wait()
                        v_copy(unit, qq).start()
                        compute(bufs[qq][0], xbufs[tt][0], unit)

        v_copy(tiles * upt - 1, 1).wait()
        pltpu.sync_copy(hid_v, hid_hbm.at[pl.ds(row0, per)])

    return experts(uv_pack, idx3, xn)


EVAL_TILES = 2


def _pack_bf16_halves(tab):
    half = tab.shape[1] // 2
    bits = lax.bitcast_convert_type(tab.astype(BF16), jnp.uint16).astype(jnp.uint32)
    return bits[:, :half] | (bits[:, half:] << 16)


def _unpack_bf16_halves(w):
    lo = lax.bitcast_convert_type(w << 16, F32)
    hi = lax.bitcast_convert_type(w & jnp.uint32(0xFFFF0000), F32)
    return lo, hi


def _eval_body(vg_ref, hid_ref, g_ref, h_ref, fw_ref, o_ref, *, slots):
    r = SUBLANES
    half = h_ref.shape[1] // 2
    for tile in range(EVAL_TILES):
        rows = slice(tile * r, (tile + 1) * r)
        act = _gelu(hid_ref[rows, :]) * g_ref[rows, :]
        acc_lo = jnp.zeros((r, half), F32)
        acc_hi = jnp.zeros((r, half), F32)
        for k in range(slots):
            v_lo, v_hi = _unpack_bf16_halves(vg_ref[tile, k * r:(k + 1) * r, :])
            a = act[:, k:k + 1]
            acc_lo = acc_lo + a * v_lo
            acc_hi = acc_hi + a * v_hi
        o_ref[rows, :] = _rms(h_ref[rows, :] + jnp.concatenate([acc_lo, acc_hi], axis=1), fw_ref[...])


def _peer_eval(vg, hid, gates, h, fw, slots):
    t, d = h.shape
    r = SUBLANES
    g = EVAL_TILES
    vg = vg.reshape(t // r, slots * r, d // 2)
    return pl.pallas_call(
        functools.partial(_eval_body, slots=slots),
        grid=(t // (r * g),),
        in_specs=[pl.BlockSpec((g, slots * r, d // 2), lambda i: (i, 0, 0)),
                  pl.BlockSpec((g * r, slots), lambda i: (i, 0)), pl.BlockSpec((g * r, slots), lambda i: (i, 0)),
                  pl.BlockSpec((g * r, d), lambda i: (i, 0)), _full(fw.shape)],
        out_specs=pl.BlockSpec((g * r, d), lambda i: (i, 0)),
        out_shape=jax.ShapeDtypeStruct((t, d), F32),
        compiler_params=_params(("parallel",)),
        name="peer_eval",
    )(vg, hid, gates, h, fw)


PEER_CHUNK_TOKENS = 2048


def _make_peer(w_q, sub_keys, u_tab, v_tab, final_norm_w):
    d = w_q.shape[0]
    heads, _, nk, dk = sub_keys.shape
    slots = heads * PEER_TOPK
    wqh, wql = _split_bf16(w_q.astype(F32))
    kh, kl = _split_bf16(sub_keys.reshape(heads * 2, nk, dk).astype(F32))
    fw = final_norm_w.reshape(1, d).astype(F32)
    uv_pack = jnp.concatenate([_pack_bf16_halves(u_tab.astype(F32)), _pack_bf16_halves(v_tab.astype(F32))],
                              axis=1)
    r = SUBLANES

    def route(xn):
        t = xn.shape[0]
        expert_t, gates_t = _peer_route(xn, wqh, wql, kh, kl, heads, nk)
        nb, _, tm = expert_t.shape
        idx = expert_t.reshape(nb, slots, tm // r, r).transpose(0, 2, 1, 3).reshape(t // r, slots * r)
        gates = gates_t.transpose(0, 2, 1).reshape(t, slots)
        return idx, gates

    def experts(routed, xn, h):
        idx, gates = routed
        t = xn.shape[0]
        tc = min(PEER_CHUNK_TOKENS, t)
        outs = []
        for c in range(t // tc):
            rows = slice(c * tc, (c + 1) * tc)
            hid, vg = _sc_experts(uv_pack, idx[c * tc // r:(c + 1) * tc // r].reshape(-1), xn[rows])
            hid = hid.reshape(tc // r, slots, r).transpose(0, 2, 1).reshape(tc, slots)
            outs.append(_peer_eval(vg, hid, gates[rows], h[rows], fw, slots))
        return outs

    return route, experts


def kernel(x, mix_norm_w, w_in, s5_a_re, s5_a_im, s5_log_dt, s5_b_re, s5_b_im, s5_c_re, s5_c_im, s5_d, w_glu, conv_w, conv_b, ssd_dt_bias, ssd_a_log, ssd_d, ssd_norm_w, w_ssd_out, w_o, ffn_norm_w, peer_w_q, peer_sub_keys, peer_u, peer_v, final_norm_w):
    bsz, seqlen, d = x.shape
    mixer = _make_mixer(mix_norm_w[0], w_in[0], s5_a_re[0], s5_a_im[0], s5_log_dt[0], s5_b_re[0], s5_b_im[0],
                        s5_c_re[0], s5_c_im[0], s5_d[0], w_glu[0], conv_w[0], conv_b[0], ssd_dt_bias[0],
                        ssd_a_log[0], ssd_d[0], ssd_norm_w[0], w_ssd_out[0], w_o[0], ffn_norm_w[0])
    route, experts = _make_peer(peer_w_q[0], peer_sub_keys[0], peer_u[0], peer_v[0], final_norm_w)
    outs = []
    for b in range(bsz):
        h, xn = mixer(x[b], 1, seqlen)
        outs += experts(route(xn), xn, h)
    return jnp.concatenate(outs, axis=0).reshape(bsz, seqlen, d).astype(x.dtype)
```

```python
import functools

import jax
import jax.numpy as jnp
from jax import lax
from jax.experimental import pallas as pl
from jax.experimental.pallas import tpu as pltpu

F32 = jnp.float32
BF16 = jnp.bfloat16

RMS_EPS = 1e-6
SUBLANES = 8
LANES = 128
SSD_CHUNK = 128
SSD_HEADDIM = 64
SSD_GROUPS = 4
SSD_STATE = 64
PEER_TOPK = 16
VMEM_LIMIT = 56 * 1024 * 1024


def _params(sem):
    return pltpu.CompilerParams(dimension_semantics=sem, vmem_limit_bytes=VMEM_LIMIT)


def _rms(x, w):
    return x * lax.rsqrt(jnp.mean(x * x, axis=-1, keepdims=True) + RMS_EPS) * w


def _dot(a, b):
    return jnp.dot(a, b, preferred_element_type=F32)


def _full(shape):
    n = len(shape)
    return pl.BlockSpec(shape, lambda *_: (0,) * n)


def _inproj_body(x_ref, nw_ref, wu_ref, wz_ref, wx_ref, wdf_ref, wdb_ref, wga_ref, wgb_ref,
                 u_ref, z_ref, xbc_ref, dtf_ref, dtb_ref, ga_ref, gb_ref):
    xb = _rms(x_ref[...], nw_ref[...]).astype(BF16)
    u_ref[...] = _dot(xb, wu_ref[...]).astype(BF16)
    z_ref[...] = _dot(xb, wz_ref[...])
    xbc_ref[...] = _dot(xb, wx_ref[...])
    dtf_ref[...] = _dot(xb, wdf_ref[...])
    dtb_ref[...] = _dot(xb, wdb_ref[...])
    ga_ref[...] = _dot(xb, wga_ref[...])
    gb_ref[...] = _dot(xb, wgb_ref[...])


def _in_proj(x2, nw, ws, tm=512):
    t, d = x2.shape
    widths = [w.shape[1] for w in ws]
    dts = [BF16] + [F32] * 6
    return pl.pallas_call(
        _inproj_body,
        grid=(t // tm,),
        in_specs=[pl.BlockSpec((tm, d), lambda i: (i, 0)), _full((1, d))] + [_full(w.shape) for w in ws],
        out_specs=[pl.BlockSpec((tm, n), lambda i: (i, 0)) for n in widths],
        out_shape=[jax.ShapeDtypeStruct((t, n), dt) for n, dt in zip(widths, dts)],
        compiler_params=_params(("parallel",)),
        name="in_proj",
    )(x2, nw, *ws)


S5_LANE_GROUP = 4


def _s5_body(u_ref, bd_ref, cd_ref, tab_ref, y_ref, s_ref, car_ref, *, reverse, lb, ns):
    @pl.when(pl.program_id(1) == 0)
    def _():
        car_ref[...] = jnp.zeros_like(car_ref)

    s_ref[...] = _dot(u_ref[...], bd_ref[...])
    nt = lb // SUBLANES
    edge = 0 if reverse else SUBLANES - 1

    for c0 in range(0, ns // LANES, S5_LANE_GROUP):
        los = [(c0 + g) * LANES for g in range(S5_LANE_GROUP)]

        def tile_step(i, carry, los=los):
            r = (nt - 1 - i) if reverse else i
            rows = pl.ds(pl.multiple_of(r * SUBLANES, SUBLANES), SUBLANES)
            new = []
            for g, lo in enumerate(los):
                re = s_ref[rows, lo:lo + LANES]
                im = s_ref[rows, ns + lo:ns + lo + LANES]
                for kk, k in enumerate((1, 2, 4)):
                    pr = tab_ref[2 * kk, :, lo:lo + LANES]
                    pi = tab_ref[2 * kk + 1, :, lo:lo + LANES]
                    sh = (SUBLANES - k) if reverse else k
                    rr = pltpu.roll(re, sh, 0)
                    ri = pltpu.roll(im, sh, 0)
                    re, im = re + pr * rr - pi * ri, im + pr * ri + pi * rr
                pr = tab_ref[6, :, lo:lo + LANES]
                pi = tab_ref[7, :, lo:lo + LANES]
                cr, ci = carry[2 * g], carry[2 * g + 1]
                re, im = re + pr * cr - pi * ci, im + pr * ci + pi * cr
                s_ref[rows, lo:lo + LANES] = re
                s_ref[rows, ns + lo:ns + lo + LANES] = im
                new.append(jnp.broadcast_to(re[edge:edge + 1, :], (SUBLANES, LANES)))
                new.append(jnp.broadcast_to(im[edge:edge + 1, :], (SUBLANES, LANES)))
            return tuple(new)

        carry0 = []
        for lo in los:
            carry0.append(car_ref[0, :, lo:lo + LANES])
            carry0.append(car_ref[1, :, lo:lo + LANES])
        carry = lax.fori_loop(0, nt, tile_step, tuple(carry0))
        for g, lo in enumerate(los):
            car_ref[0, :, lo:lo + LANES] = carry[2 * g]
            car_ref[1, :, lo:lo + LANES] = carry[2 * g + 1]

    y_ref[...] = _dot(s_ref[...].astype(BF16), cd_ref[...])


def _s5_dir(u, bd, cd, tab, bsz, seqlen, reverse, lb=256):
    t, w = u.shape
    ns = bd.shape[1] // 2
    nb = seqlen // lb

    def blk(b, j):
        return (b * nb + ((nb - 1 - j) if reverse else j), 0)

    return pl.pallas_call(
        functools.partial(_s5_body, reverse=reverse, lb=lb, ns=ns),
        grid=(bsz, nb),
        in_specs=[pl.BlockSpec((lb, w), blk), _full(bd.shape), _full(cd.shape), _full(tab.shape)],
        out_specs=pl.BlockSpec((lb, w), blk),
        out_shape=jax.ShapeDtypeStruct((t, w), F32),
        scratch_shapes=[pltpu.VMEM((lb, 2 * ns), F32), pltpu.VMEM((2, SUBLANES, ns), F32)],
        compiler_params=_params(("parallel", "arbitrary")),
        name="s5_bwd" if reverse else "s5_fwd",
    )(u, bd, cd, tab)


def _s5_prep(a_re, a_im, log_dt, b_re, b_im, c_re, c_im, reverse):
    g, p = a_re.shape
    c = b_re.shape[-1]
    ar = a_re.astype(F32)
    ai = a_im.astype(F32)
    step = jnp.exp(log_dt.astype(F32))[:, None]
    mag = jnp.exp(ar * step)
    lam_r = mag * jnp.cos(ai * step)
    lam_i = mag * jnp.sin(ai * step)
    den = ar * ar + ai * ai
    f_r = ((lam_r - 1.0) * ar + lam_i * ai) / den
    f_i = (lam_i * ar - (lam_r - 1.0) * ai) / den
    bt_r = f_r[..., None] * b_re - f_i[..., None] * b_im
    bt_i = f_r[..., None] * b_im + f_i[..., None] * b_re
    eye = jnp.eye(g, dtype=F32)
    bd = jnp.concatenate([jnp.einsum('gpc,gh->gchp', bt_r, eye).reshape(g * c, g * p),
                          jnp.einsum('gpc,gh->gchp', bt_i, eye).reshape(g * c, g * p)], axis=1)
    cd = jnp.concatenate([jnp.einsum('gcp,gh->gphc', c_re, eye).reshape(g * p, g * c),
                          -jnp.einsum('gcp,gh->gphc', c_im, eye).reshape(g * p, g * c)], axis=0)
    lr = lam_r.reshape(1, g * p)
    li = lam_i.reshape(1, g * p)
    pw = [(lr, li)]
    for _ in range(SUBLANES - 1):
        qr, qi = pw[-1]
        pw.append((qr * lr - qi * li, qr * li + qi * lr))
    row = jnp.arange(SUBLANES)[:, None]
    tabs = []
    for k in (1, 2, 4):
        keep = (row < SUBLANES - k) if reverse else (row >= k)
        for comp in pw[k - 1]:
            tabs.append(jnp.where(keep, comp, 0.0))
    order = list(range(SUBLANES - 1, -1, -1)) if reverse else list(range(SUBLANES))
    for comp in (0, 1):
        tabs.append(jnp.concatenate([pw[k][comp] for k in order], axis=0))
    return bd.astype(BF16), cd.astype(BF16), jnp.stack(tabs, axis=0)


def _conv_body(cur_ref, prev_ref, nxt_ref, w_ref, b_ref, xs_ref, bc_ref, *, nblk, width, d_inner):
    j = pl.program_id(0) % nblk
    pad = width // 2
    cur = cur_ref[...]
    prev = jnp.where(j == 0, 0.0, prev_ref[...])
    nxt = jnp.where(j == nblk - 1, 0.0, nxt_ref[...])
    ext = jnp.concatenate([prev, cur, nxt], axis=0)
    tm = cur.shape[0]
    acc = jnp.zeros_like(cur) + b_ref[...]
    for k in range(width):
        off = SUBLANES - pad + k
        acc = acc + ext[off:off + tm, :] * w_ref[k:k + 1, :]
    act = acc * jax.nn.sigmoid(acc)
    xs_ref[...] = act[:, :d_inner]
    bc_ref[...] = act[:, d_inner:].astype(BF16)


def _conv(xbc, w, b, seqlen, d_inner, tm=256):
    t, cdim = xbc.shape
    width = w.shape[0]
    nblk = seqlen // tm
    r = tm // SUBLANES
    last = t // SUBLANES - 1
    return pl.pallas_call(
        functools.partial(_conv_body, nblk=nblk, width=width, d_inner=d_inner),
        grid=(t // tm,),
        in_specs=[pl.BlockSpec((tm, cdim), lambda i: (i, 0)),
                  pl.BlockSpec((SUBLANES, cdim), lambda i: (jnp.maximum(i * r - 1, 0), 0)),
                  pl.BlockSpec((SUBLANES, cdim), lambda i: (jnp.minimum((i + 1) * r, last), 0)),
                  _full(w.shape), _full(b.shape)],
        out_specs=[pl.BlockSpec((tm, d_inner), lambda i: (i, 0)),
                   pl.BlockSpec((tm, cdim - d_inner), lambda i: (i, 0))],
        out_shape=[jax.ShapeDtypeStruct((t, d_inner), F32),
                   jax.ShapeDtypeStruct((t, cdim - d_inner), BF16)],
        compiler_params=_params(("parallel",)),
        name="ssd_conv",
    )(xbc, xbc, xbc, w, b)


def _cumsum_rows(x):
    n = x.shape[0]
    ri = lax.broadcasted_iota(jnp.int32, x.shape, 0)
    k = 1
    while k < n:
        x = x + jnp.where(ri >= k, pltpu.roll(x, k, 0), 0.0)
        k *= 2
    return x


def _softplus(x):
    return jnp.maximum(x, 0.0) + jnp.log1p(jnp.exp(-jnp.abs(x)))


def _ssd_body(xs_ref, bc_ref, dt_ref, bias_ref, a_ref, y_ref, st_ref, *, reverse, heads):
    @pl.when(pl.program_id(1) == 0)
    def _():
        st_ref[...] = jnp.zeros_like(st_ref)

    n = SSD_CHUNK
    p = SSD_HEADDIM
    ng = SSD_GROUPS
    ns = SSD_STATE
    hg = heads // ng
    dt = _softplus(dt_ref[...] + bias_ref[...])
    adt = dt * a_ref[...]
    cs = _cumsum_rows(adt)
    tot = cs[n - 1:n, :]
    if reverse:
        cs = tot - cs + adt
    cs_t = cs.T
    dt_t = dt.T
    ecs = jnp.exp(cs)
    dst = jnp.exp(tot - cs) * dt
    etot = jnp.exp(tot)
    li = lax.broadcasted_iota(jnp.int32, (n, n), 0)
    si = lax.broadcasted_iota(jnp.int32, (n, n), 1)
    mask = (si >= li) if reverse else (si <= li)
    for g in range(ng):
        bg = bc_ref[:, g * ns:(g + 1) * ns]
        cg = bc_ref[:, ng * ns + g * ns:ng * ns + (g + 1) * ns]
        cb = lax.dot_general(cg, bg, (((1,), (1,)), ((), ())), preferred_element_type=F32)
        for hh in range(hg):
            h = g * hg + hh
            xh = xs_ref[:, h * p:(h + 1) * p]
            lm = jnp.exp(jnp.where(mask, cs[:, h:h + 1] - cs_t[h:h + 1, :], -jnp.inf))
            m = (cb * lm * dt_t[h:h + 1, :]).astype(BF16)
            yd = _dot(m, xh.astype(BF16))
            sp = st_ref[h]
            yo = _dot(cg, sp.astype(BF16)) * ecs[:, h:h + 1]
            y_ref[:, h * p:(h + 1) * p] = yd + yo
            xw = (xh * dst[:, h:h + 1]).astype(BF16)
            stn = lax.dot_general(bg, xw, (((0,), (0,)), ((), ())), preferred_element_type=F32)
            st_ref[h] = sp * etot[:, h:h + 1] + stn


def _ssd_dir(xs, bc, dt, bias, a, bsz, seqlen, heads, reverse):
    t, d_inner = xs.shape
    n = SSD_CHUNK
    nc = seqlen // n

    def blk(b, j):
        return (b * nc + ((nc - 1 - j) if reverse else j), 0)

    return pl.pallas_call(
        functools.partial(_ssd_body, reverse=reverse, heads=heads),
        grid=(bsz, nc),
        in_specs=[pl.BlockSpec((n, d_inner), blk), pl.BlockSpec((n, bc.shape[1]), blk),
                  pl.BlockSpec((n, LANES), blk), _full(bias.shape), _full(a.shape)],
        out_specs=pl.BlockSpec((n, d_inner), blk),
        out_shape=jax.ShapeDtypeStruct((t, d_inner), F32),
        scratch_shapes=[pltpu.VMEM((heads, SSD_STATE, SSD_HEADDIM), F32)],
        compiler_params=_params(("parallel", "arbitrary")),
        name="ssd_bwd" if reverse else "ssd_fwd",
    )(xs, bc, dt, bias, a)


def _gelu(x):
    return 0.5 * x * (1.0 + lax.erf(x * (2.0 ** -0.5)))


def _merge_body(x_ref, u_ref, y5f_ref, y5b_ref, d5_ref, wglu_ref, ga_ref,
                xs_ref, ysf_ref, ysb_ref, dss_ref, z_ref, nw_ref, wso_ref, gb_ref, wo_ref, fnw_ref,
                h_ref, xn_ref, *, d_model, groups):
    ya = u_ref[...].astype(F32) * d5_ref[...] + y5f_ref[...] + y5b_ref[...]
    glu = _dot(_gelu(ya).astype(BF16), wglu_ref[...])
    ya = glu[:, :d_model] * jax.nn.sigmoid(glu[:, d_model:])
    z = z_ref[...]
    yb = (xs_ref[...] * dss_ref[...] + ysf_ref[...] + ysb_ref[...]) * (z * jax.nn.sigmoid(z))
    gw = yb.shape[1] // groups
    parts = []
    for g in range(groups):
        yg = yb[:, g * gw:(g + 1) * gw]
        parts.append(yg * lax.rsqrt(jnp.mean(yg * yg, axis=-1, keepdims=True) + RMS_EPS))
    yb = jnp.concatenate(parts, axis=1) * nw_ref[...]
    yb = _dot(yb.astype(BF16), wso_ref[...])
    merged = jax.nn.sigmoid(ga_ref[...]) * ya + jax.nn.sigmoid(gb_ref[...]) * yb
    h = x_ref[...] + _dot(merged.astype(BF16), wo_ref[...])
    h_ref[...] = h
    xn_ref[...] = _rms(h, fnw_ref[...])


def _merge(x2, u, y5f, y5b, d5, wglu, ga, xs, ysf, ysb, dss, z, nw, wso, gb, wo, fnw, tm=256):
    t, d = x2.shape
    args = [x2, u, y5f, y5b, d5, wglu, ga, xs, ysf, ysb, dss, z, nw, wso, gb, wo, fnw]
    per_token = [True, True, True, True, False, False, True, True, True, True, False, True, False, False, True,
                 False, False]
    specs = []
    for a, tok in zip(args, per_token):
        if tok:
            specs.append(pl.BlockSpec((tm, a.shape[1]), lambda i: (i, 0)))
        else:
            specs.append(_full(a.shape))
    return pl.pallas_call(
        functools.partial(_merge_body, d_model=d, groups=SSD_GROUPS),
        grid=(t // tm,),
        in_specs=specs,
        out_specs=[pl.BlockSpec((tm, d), lambda i: (i, 0))] * 2,
        out_shape=[jax.ShapeDtypeStruct((t, d), F32)] * 2,
        compiler_params=_params(("parallel",)),
        name="merge",
    )(*args)


def _make_mixer(mix_norm_w, w_in, s5_a_re, s5_a_im, s5_log_dt, s5_b_re, s5_b_im, s5_c_re, s5_c_im,
                s5_d, w_glu, conv_w, conv_b, ssd_dt_bias, ssd_a_log, ssd_d, ssd_norm_w, w_ssd_out, w_o, ffn_norm_w):
    d = w_in.shape[0]
    s5w = s5_d.shape[0]
    d_inner = w_ssd_out.shape[0]
    heads = ssd_d.shape[0]
    cdim = conv_b.shape[0]
    o1 = s5w
    o2 = o1 + d_inner
    o3 = o2 + cdim
    o4 = o3 + 2 * heads

    def padw(w):
        return jnp.pad(w, ((0, 0), (0, LANES - w.shape[1])))

    ws = [w_in[:, :o1], w_in[:, o1:o2], w_in[:, o2:o3], padw(w_in[:, o3:o3 + heads]),
          padw(w_in[:, o3 + heads:o4]), w_in[:, o4:o4 + d], w_in[:, o4 + d:]]
    ws = [w.astype(BF16) for w in ws]
    nw = mix_norm_w.reshape(1, d).astype(F32)
    s5p = [_s5_prep(s5_a_re[k], s5_a_im[k], s5_log_dt[k], s5_b_re[k], s5_b_im[k], s5_c_re[k], s5_c_im[k],
                    bool(k)) for k in (0, 1)]
    cw = conv_w.reshape(conv_w.shape[0], cdim).astype(F32)
    cb = conv_b.reshape(1, cdim).astype(F32)
    bias = [jnp.pad(ssd_dt_bias[k].astype(F32), (0, LANES - heads)).reshape(1, LANES) for k in (0, 1)]
    a_neg = [jnp.pad(-jnp.exp(ssd_a_log[k].astype(F32)), (0, LANES - heads)).reshape(1, LANES) for k in (0, 1)]
    dss = jnp.repeat(ssd_d.astype(F32), d_inner // heads).reshape(1, d_inner)
    d5 = s5_d.reshape(1, s5w).astype(F32)
    wglu = w_glu.astype(BF16)
    snw = ssd_norm_w.reshape(1, d_inner).astype(F32)
    wso = w_ssd_out.astype(BF16)
    wo = w_o.astype(BF16)
    fnw = ffn_norm_w.reshape(1, d).astype(F32)

    def run(x2, bsz, seqlen):
        u, z, xbc, dtf, dtb, ga, gb = _in_proj(x2, nw, ws)
        y5 = [_s5_dir(u, *s5p[k], bsz, seqlen, bool(k)) for k in (0, 1)]
        xs, bc = _conv(xbc, cw, cb, seqlen, d_inner)
        ys = [_ssd_dir(xs, bc, (dtf, dtb)[k], bias[k], a_neg[k], bsz, seqlen, heads, bool(k)) for k in (0, 1)]
        return _merge(x2, u, y5[0], y5[1], d5, wglu, ga, xs, ys[0], ys[1], dss, z, snw, wso, gb, wo, fnw)

    return run


def _split_bf16(x):
    hi = x.astype(BF16)
    return hi, (x - hi.astype(F32)).astype(BF16)


def _dot3(ah, al, bh, bl, dims=(((1,), (0,)), ((), ()))):
    def dg(p, q):
        return lax.dot_general(p, q, dims, preferred_element_type=F32)
    return dg(ah, bh) + (dg(al, bh) + dg(ah, bl))


def _topk_rows(xs, k, payloads=None):
    n = xs[0].shape[0]
    ri = lax.broadcasted_iota(jnp.int32, xs[0].shape, 0)
    xs = list(xs)
    vals = [[] for _ in xs]
    outs = [[] for _ in xs]
    for _ in range(k):
        for i, x in enumerate(xs):
            m = jnp.max(x, axis=0, keepdims=True)
            pos = jnp.min(jnp.where(x == m, ri, n), axis=0, keepdims=True)
            sel = ri == pos
            vals[i].append(m)
            if payloads is None:
                outs[i].append(pos)
            else:
                outs[i].append(jnp.max(jnp.where(sel, payloads[i], -1), axis=0, keepdims=True))
            xs[i] = jnp.where(sel, -jnp.inf, x)
    return [(jnp.concatenate(v, axis=0), jnp.concatenate(o, axis=0)) for v, o in zip(vals, outs)]


def _pair_candidates(c1, c2, combine):
    r = SUBLANES
    top = lax.broadcasted_iota(jnp.int32, (r, c1.shape[1]), 0) < r // 2
    lo = c2[0:r]
    quad = jnp.concatenate([c2[0:r // 2], c2[0:r // 2]], axis=0)
    blocks = [combine(c1[0:1], lo), combine(c1[0:1], c2[r:2 * r])]
    blocks += [combine(c1[a:a + 1], lo) for a in (1, 2, 3)]
    blocks += [combine(jnp.where(top, c1[a:a + 1], c1[a + 1:a + 2]), quad) for a in (4, 6)]
    blocks.append(combine(c1[r:2 * r], c2[0:1]))
    return jnp.concatenate(blocks, axis=0)


def _route_body(xn_ref, wqh_ref, wql_ref, kh_ref, kl_ref, e_ref, g_ref, q_ref, *, heads, nk, dk):
    k = PEER_TOPK
    xh, xl = _split_bf16(xn_ref[...])
    q_ref[...] = _dot3(xh, xl, wqh_ref[...], wql_ref[...])
    nt = (((1,), (1,)), ((), ()))

    def head_pair(hp, _):
        cands, ids = [], []
        for hh in range(2):
            h = hp * 2 + hh
            scores = []
            for s in range(2):
                col = pl.multiple_of((h * 2 + s) * dk, dk)
                qh, ql = _split_bf16(q_ref[:, pl.ds(col, dk)])
                scores.append(_dot3(kh_ref[h * 2 + s], kl_ref[h * 2 + s], qh, ql, nt))
            (s1, i1), (s2, i2) = _topk_rows(scores, k)
            cands.append(_pair_candidates(s1, s2, lambda a, b: a + b))
            ids.append(_pair_candidates(i1, i2, lambda a, b: a * nk + b))
        picks = _topk_rows(cands, k, ids)
        for hh, (best, expert) in enumerate(picks):
            w = jnp.exp(best - best[0:1, :])
            gate = w / jnp.sum(w, axis=0, keepdims=True)
            rows = pl.ds(pl.multiple_of((hp * 2 + hh) * k, k), k)
            e_ref[0, rows, :] = expert
            g_ref[0, rows, :] = gate
        return 0

    lax.fori_loop(0, heads // 2, head_pair, 0)


def _peer_route(xn, tok_base, ntok, wqh, wql, kh, kl, heads, nk, tm=LANES):
    d = xn.shape[1]
    dk = kh.shape[-1]
    nb = ntok // tm
    b0 = tok_base // tm
    slots = heads * PEER_TOPK
    return pl.pallas_call(
        functools.partial(_route_body, heads=heads, nk=nk, dk=dk),
        grid=(nb,),
        in_specs=[pl.BlockSpec((tm, d), lambda i: (i + b0, 0)), _full(wqh.shape), _full(wql.shape),
                  _full(kh.shape), _full(kl.shape)],
        out_specs=[pl.BlockSpec((1, slots, tm), lambda i: (i, 0, 0))] * 2,
        out_shape=[jax.ShapeDtypeStruct((nb, slots, tm), jnp.int32),
                   jax.ShapeDtypeStruct((nb, slots, tm), F32)],
        scratch_shapes=[pltpu.VMEM((tm, wqh.shape[1]), F32)],
        compiler_params=_params(("parallel",)),
        name="peer_route",
    )(xn, wqh, wql, kh, kl)


SC_CORES = 2
SC_SUBCORES = 16
SC_LANES = 16
ROW_UNIT = 32


def _sc_experts(uv_pack, idx, xn, tok_base, slots):
    from jax.experimental.pallas import tpu_sc as plsc
    r = SUBLANES
    n = idx.shape[0]
    d = xn.shape[1]
    half = d // 2
    workers = SC_CORES * SC_SUBCORES
    per = n // workers
    tile_rows = slots * r
    tiles = per // tile_rows
    upt = tile_rows // ROW_UNIT
    nl = SC_LANES
    idx3 = idx.reshape(workers, tiles * upt, ROW_UNIT)
    mesh = plsc.VectorSubcoreMesh(core_axis_name="c", subcore_axis_name="s")

    @functools.partial(
        pl.kernel, mesh=mesh,
        out_type=[jax.ShapeDtypeStruct((n,), F32), jax.ShapeDtypeStruct((n, half), jnp.uint32)],
        scratch_types=[pltpu.VMEM((tiles * upt, ROW_UNIT), jnp.int32),
                       pltpu.VMEM((ROW_UNIT, d), jnp.uint32), pltpu.VMEM((ROW_UNIT, d), jnp.uint32),
                       pltpu.VMEM((r, d), F32), pltpu.VMEM((r, d), F32),
                       pltpu.VMEM((per,), F32),
                       pltpu.SemaphoreType.DMA, pltpu.SemaphoreType.DMA,
                       pltpu.SemaphoreType.DMA, pltpu.SemaphoreType.DMA,
                       pltpu.SemaphoreType.DMA, pltpu.SemaphoreType.DMA],
        compiler_params=pltpu.CompilerParams(needs_layout_passes=False),
        name="peer_experts",
    )
    def experts(uv_hbm, idx_hbm, x_hbm, hid_hbm, vst_hbm, idx_v, rows_a, rows_b, x_a, x_b, hid_v,
                ga, gb, wa, wb, xa, xb):
        wid = lax.axis_index("s") * SC_CORES + lax.axis_index("c")
        row0 = wid * per
        tok0 = tok_base + wid * tiles * r
        pltpu.sync_copy(idx_hbm.at[wid], idx_v)
        bufs = ((rows_a, ga, wa), (rows_b, gb, wb))
        xbufs = ((x_a, xa), (x_b, xb))

        def rows_copy(unit, which):
            buf, sem, _ = bufs[which]
            return pltpu.make_async_copy(uv_hbm.at[idx_v.at[unit]], buf, sem)

        def v_copy(unit, which):
            buf, _, sem = bufs[which]
            return pltpu.make_async_copy(buf.at[:, pl.ds(half, half)],
                                         vst_hbm.at[pl.ds(row0 + unit * ROW_UNIT, ROW_UNIT)], sem)

        def x_copy(tile, which):
            buf, sem = xbufs[which]
            return pltpu.make_async_copy(x_hbm.at[pl.ds(tok0 + tile * r, r)], buf, sem)

        def compute(rows_ref, x_ref, unit):
            lane = lax.iota(jnp.int32, nl)
            for grp in range(ROW_UNIT // nl):
                def body(j, acc, grp=grp):
                    off = pl.multiple_of(j * nl, nl)
                    x_lo = [x_ref[tk, pl.ds(off, nl)] for tk in range(r)]
                    x_hi = [x_ref[tk, pl.ds(half + off, nl)] for tk in range(r)]
                    new = []
                    for i in range(nl):
                        w = rows_ref[grp * nl + i, pl.ds(off, nl)]
                        lo = lax.bitcast_convert_type(w << 16, F32)
                        hi = lax.bitcast_convert_type(w & jnp.uint32(0xFFFF0000), F32)
                        new.append(acc[i] + lo * x_lo[i % r] + hi * x_hi[i % r])
                    return tuple(new)

                acc = lax.fori_loop(0, half // nl, body, tuple(jnp.zeros((nl,), F32) for _ in range(nl)))
                res = jnp.zeros((nl,), F32)
                for i in range(nl):
                    res = jnp.where(lane == i, jnp.sum(acc[i]), res)
                tile = unit // upt
                p = (unit - tile * upt) * ROW_UNIT + grp * nl + lane
                dest = tile * tile_rows + (p & (r - 1)) * slots + (p >> (r.bit_length() - 1))
                plsc.store_scatter(hid_v, [dest], res)

        rows_copy(0, 0).start()
        x_copy(0, 0).start()

        @pl.loop(0, tiles, step=2)
        def _(tile0):
            for tt in range(2):
                tile = tile0 + tt
                x_copy(tile, tt).wait()

                @pl.when(tile + 1 < tiles)
                def _():
                    x_copy(tile + 1, 1 - tt).start()

                @pl.loop(0, upt, step=2)
                def _(q0):
                    for qq in range(2):
                        unit = tile * upt + q0 + qq
                        @pl.when(unit > 0)
                        def _():
                            v_copy(unit - 1, 1 - qq).wait()

                        @pl.when(unit + 1 < tiles * upt)
                        def _():
                            rows_copy(unit + 1, 1 - qq).start()

                        rows_copy(unit, qq).wait()
                        v_copy(unit, qq).start()
                        compute(bufs[qq][0], xbufs[tt][0], unit)

        v_copy(tiles * upt - 1, 1).wait()
        pltpu.sync_copy(hid_v, hid_hbm.at[pl.ds(row0, per)])

    return experts(uv_pack, idx3, xn)


EVAL_TILES = 2


def _pack_bf16_halves(tab):
    half = tab.shape[1] // 2
    bits = lax.bitcast_convert_type(tab.astype(BF16), jnp.uint16).astype(jnp.uint32)
    return bits[:, :half] | (bits[:, half:] << 16)


def _unpack_bf16_halves(w):
    lo = lax.bitcast_convert_type(w << 16, F32)
    hi = lax.bitcast_convert_type(w & jnp.uint32(0xFFFF0000), F32)
    return lo, hi


def _eval_body(vg_ref, hid_ref, g_ref, h_ref, fw_ref, o_ref, *, slots):
    r = SUBLANES
    half = h_ref.shape[1] // 2
    for tile in range(EVAL_TILES):
        rows = slice(tile * r, (tile + 1) * r)
        act = _gelu(hid_ref[rows, :]) * g_ref[rows, :]
        acc_lo = jnp.zeros((r, half), F32)
        acc_hi = jnp.zeros((r, half), F32)
        for k in range(slots):
            v_lo, v_hi = _unpack_bf16_halves(vg_ref[tile, k * r:(k + 1) * r, :])
            a = act[:, k:k + 1]
            acc_lo = acc_lo + a * v_lo
            acc_hi = acc_hi + a * v_hi
        o_ref[rows, :] = _rms(h_ref[rows, :] + jnp.concatenate([acc_lo, acc_hi], axis=1), fw_ref[...])


def _peer_eval(vg, hid, gates, h, tok_base, fw, slots):
    t = hid.shape[0]
    d = h.shape[1]
    r = SUBLANES
    g = EVAL_TILES
    s0 = tok_base // (r * g)
    vg = vg.reshape(t // r, slots * r, d // 2)
    return pl.pallas_call(
        functools.partial(_eval_body, slots=slots),
        grid=(t // (r * g),),
        in_specs=[pl.BlockSpec((g, slots * r, d // 2), lambda i: (i, 0, 0)),
                  pl.BlockSpec((g * r, slots), lambda i: (i, 0)), pl.BlockSpec((g * r, slots), lambda i: (i, 0)),
                  pl.BlockSpec((g * r, d), lambda i: (i + s0, 0)), _full(fw.shape)],
        out_specs=pl.BlockSpec((g * r, d), lambda i: (i, 0)),
        out_shape=jax.ShapeDtypeStruct((t, d), F32),
        compiler_params=_params(("parallel",)),
        name="peer_eval",
    )(vg, hid, gates, h, fw)


PEER_CHUNK_TOKENS = 2048
PEER_CHUNKS_AHEAD = 2


def _make_peer(w_q, sub_keys, u_tab, v_tab, final_norm_w):
    d = w_q.shape[0]
    heads, _, nk, dk = sub_keys.shape
    slots = heads * PEER_TOPK
    wqh, wql = _split_bf16(w_q.astype(F32))
    kh, kl = _split_bf16(sub_keys.reshape(heads * 2, nk, dk).astype(F32))
    fw = final_norm_w.reshape(1, d).astype(F32)
    uv_pack = jnp.concatenate([_pack_bf16_halves(u_tab.astype(F32)), _pack_bf16_halves(v_tab.astype(F32))],
                              axis=1)
    r = SUBLANES

    def chunk(xn, h, tok_base, ntok, after):
        expert_t, gates_t = _peer_route(xn, tok_base, ntok, wqh, wql, kh, kl, heads, nk)
        nb, _, tm = expert_t.shape
        idx = expert_t.reshape(nb, slots, tm // r, r).transpose(0, 2, 1, 3).reshape(-1)
        gates = gates_t.transpose(0, 2, 1).reshape(ntok, slots)
        if after is not None:
            idx, after = lax.optimization_barrier((idx, after))
        hid, vg = _sc_experts(uv_pack, idx, xn, tok_base, slots)
        return _peer_eval(vg, hid.reshape(ntok, slots), gates, h, tok_base, fw, slots), after

    return chunk


def kernel(x, mix_norm_w, w_in, s5_a_re, s5_a_im, s5_log_dt, s5_b_re, s5_b_im, s5_c_re, s5_c_im, s5_d, w_glu, conv_w, conv_b, ssd_dt_bias, ssd_a_log, ssd_d, ssd_norm_w, w_ssd_out, w_o, ffn_norm_w, peer_w_q, peer_sub_keys, peer_u, peer_v, final_norm_w):
    bsz, seqlen, d = x.shape
    mixer = _make_mixer(mix_norm_w[0], w_in[0], s5_a_re[0], s5_a_im[0], s5_log_dt[0], s5_b_re[0], s5_b_im[0],
                        s5_c_re[0], s5_c_im[0], s5_d[0], w_glu[0], conv_w[0], conv_b[0], ssd_dt_bias[0],
                        ssd_a_log[0], ssd_d[0], ssd_norm_w[0], w_ssd_out[0], w_o[0], ffn_norm_w[0])
    peer_chunk = _make_peer(peer_w_q[0], peer_sub_keys[0], peer_u[0], peer_v[0], final_norm_w)
    tc = min(PEER_CHUNK_TOKENS, seqlen)
    outs = []
    for b in range(bsz):
        h, xn = mixer(x[b], 1, seqlen)
        for c in range(seqlen // tc):
            k = len(outs) - PEER_CHUNKS_AHEAD
            out, done = peer_chunk(xn, h, c * tc, tc, outs[k] if k >= 0 else None)
            if k >= 0:
                outs[k] = done
            outs.append(out)
    return jnp.concatenate(outs, axis=0).reshape(bsz, seqlen, d).astype(x.dtype)
```

```python
import functools

import jax
import jax.numpy as jnp
from jax import lax
from jax.experimental import pallas as pl
from jax.experimental.pallas import tpu as pltpu

F32 = jnp.float32
BF16 = jnp.bfloat16

RMS_EPS = 1e-6
SUBLANES = 8
LANES = 128
SSD_CHUNK = 128
SSD_HEADDIM = 64
SSD_GROUPS = 4
SSD_STATE = 64
PEER_TOPK = 16
VMEM_LIMIT = 56 * 1024 * 1024


def _params(sem):
    return pltpu.CompilerParams(dimension_semantics=sem, vmem_limit_bytes=VMEM_LIMIT)


def _rms(x, w):
    return x * lax.rsqrt(jnp.mean(x * x, axis=-1, keepdims=True) + RMS_EPS) * w


def _dot(a, b):
    return jnp.dot(a, b, preferred_element_type=F32)


def _full(shape):
    n = len(shape)
    return pl.BlockSpec(shape, lambda *_: (0,) * n)


def _inproj_body(x_ref, nw_ref, wu_ref, wz_ref, wx_ref, wdf_ref, wdb_ref, wga_ref, wgb_ref,
                 u_ref, z_ref, xbc_ref, dtf_ref, dtb_ref, ga_ref, gb_ref):
    xb = _rms(x_ref[...], nw_ref[...]).astype(BF16)
    u_ref[...] = _dot(xb, wu_ref[...]).astype(BF16)
    z_ref[...] = _dot(xb, wz_ref[...])
    xbc_ref[...] = _dot(xb, wx_ref[...])
    dtf_ref[...] = _dot(xb, wdf_ref[...])
    dtb_ref[...] = _dot(xb, wdb_ref[...])
    ga_ref[...] = _dot(xb, wga_ref[...])
    gb_ref[...] = _dot(xb, wgb_ref[...])


def _in_proj(x2, nw, ws, tm=512):
    t, d = x2.shape
    widths = [w.shape[1] for w in ws]
    dts = [BF16] + [F32] * 6
    return pl.pallas_call(
        _inproj_body,
        grid=(t // tm,),
        in_specs=[pl.BlockSpec((tm, d), lambda i: (i, 0)), _full((1, d))] + [_full(w.shape) for w in ws],
        out_specs=[pl.BlockSpec((tm, n), lambda i: (i, 0)) for n in widths],
        out_shape=[jax.ShapeDtypeStruct((t, n), dt) for n, dt in zip(widths, dts)],
        compiler_params=_params(("parallel",)),
        name="in_proj",
    )(x2, nw, *ws)


S5_LANE_GROUP = 4


def _s5_body(u_ref, bd_ref, cd_ref, tab_ref, y_ref, s_ref, car_ref, *, reverse, lb, ns):
    @pl.when(pl.program_id(1) == 0)
    def _():
        car_ref[...] = jnp.zeros_like(car_ref)

    s_ref[...] = _dot(u_ref[...], bd_ref[...])
    nt = lb // SUBLANES
    edge = 0 if reverse else SUBLANES - 1

    for c0 in range(0, ns // LANES, S5_LANE_GROUP):
        los = [(c0 + g) * LANES for g in range(S5_LANE_GROUP)]

        def tile_step(i, carry, los=los):
            r = (nt - 1 - i) if reverse else i
            rows = pl.ds(pl.multiple_of(r * SUBLANES, SUBLANES), SUBLANES)
            new = []
            for g, lo in enumerate(los):
                re = s_ref[rows, lo:lo + LANES]
                im = s_ref[rows, ns + lo:ns + lo + LANES]
                for kk, k in enumerate((1, 2, 4)):
                    pr = tab_ref[2 * kk, :, lo:lo + LANES]
                    pi = tab_ref[2 * kk + 1, :, lo:lo + LANES]
                    sh = (SUBLANES - k) if reverse else k
                    rr = pltpu.roll(re, sh, 0)
                    ri = pltpu.roll(im, sh, 0)
                    re, im = re + pr * rr - pi * ri, im + pr * ri + pi * rr
                pr = tab_ref[6, :, lo:lo + LANES]
                pi = tab_ref[7, :, lo:lo + LANES]
                cr, ci = carry[2 * g], carry[2 * g + 1]
                re, im = re + pr * cr - pi * ci, im + pr * ci + pi * cr
                s_ref[rows, lo:lo + LANES] = re
                s_ref[rows, ns + lo:ns + lo + LANES] = im
                new.append(jnp.broadcast_to(re[edge:edge + 1, :], (SUBLANES, LANES)))
                new.append(jnp.broadcast_to(im[edge:edge + 1, :], (SUBLANES, LANES)))
            return tuple(new)

        carry0 = []
        for lo in los:
            carry0.append(car_ref[0, :, lo:lo + LANES])
            carry0.append(car_ref[1, :, lo:lo + LANES])
        carry = lax.fori_loop(0, nt, tile_step, tuple(carry0))
        for g, lo in enumerate(los):
            car_ref[0, :, lo:lo + LANES] = carry[2 * g]
            car_ref[1, :, lo:lo + LANES] = carry[2 * g + 1]

    y_ref[...] = _dot(s_ref[...].astype(BF16), cd_ref[...])


def _s5_dir(u, bd, cd, tab, bsz, seqlen, reverse, lb=256):
    t, w = u.shape
    ns = bd.shape[1] // 2
    nb = seqlen // lb

    def blk(b, j):
        return (b * nb + ((nb - 1 - j) if reverse else j), 0)

    return pl.pallas_call(
        functools.partial(_s5_body, reverse=reverse, lb=lb, ns=ns),
        grid=(bsz, nb),
        in_specs=[pl.BlockSpec((lb, w), blk), _full(bd.shape), _full(cd.shape), _full(tab.shape)],
        out_specs=pl.BlockSpec((lb, w), blk),
        out_shape=jax.ShapeDtypeStruct((t, w), F32),
        scratch_shapes=[pltpu.VMEM((lb, 2 * ns), F32), pltpu.VMEM((2, SUBLANES, ns), F32)],
        compiler_params=_params(("parallel", "arbitrary")),
        name="s5_bwd" if reverse else "s5_fwd",
    )(u, bd, cd, tab)


def _s5_prep(a_re, a_im, log_dt, b_re, b_im, c_re, c_im, reverse):
    g, p = a_re.shape
    c = b_re.shape[-1]
    ar = a_re.astype(F32)
    ai = a_im.astype(F32)
    step = jnp.exp(log_dt.astype(F32))[:, None]
    mag = jnp.exp(ar * step)
    lam_r = mag * jnp.cos(ai * step)
    lam_i = mag * jnp.sin(ai * step)
    den = ar * ar + ai * ai
    f_r = ((lam_r - 1.0) * ar + lam_i * ai) / den
    f_i = (lam_i * ar - (lam_r - 1.0) * ai) / den
    bt_r = f_r[..., None] * b_re - f_i[..., None] * b_im
    bt_i = f_r[..., None] * b_im + f_i[..., None] * b_re
    eye = jnp.eye(g, dtype=F32)
    bd = jnp.concatenate([jnp.einsum('gpc,gh->gchp', bt_r, eye).reshape(g * c, g * p),
                          jnp.einsum('gpc,gh->gchp', bt_i, eye).reshape(g * c, g * p)], axis=1)
    cd = jnp.concatenate([jnp.einsum('gcp,gh->gphc', c_re, eye).reshape(g * p, g * c),
                          -jnp.einsum('gcp,gh->gphc', c_im, eye).reshape(g * p, g * c)], axis=0)
    lr = lam_r.reshape(1, g * p)
    li = lam_i.reshape(1, g * p)
    pw = [(lr, li)]
    for _ in range(SUBLANES - 1):
        qr, qi = pw[-1]
        pw.append((qr * lr - qi * li, qr * li + qi * lr))
    row = jnp.arange(SUBLANES)[:, None]
    tabs = []
    for k in (1, 2, 4):
        keep = (row < SUBLANES - k) if reverse else (row >= k)
        for comp in pw[k - 1]:
            tabs.append(jnp.where(keep, comp, 0.0))
    order = list(range(SUBLANES - 1, -1, -1)) if reverse else list(range(SUBLANES))
    for comp in (0, 1):
        tabs.append(jnp.concatenate([pw[k][comp] for k in order], axis=0))
    return bd.astype(BF16), cd.astype(BF16), jnp.stack(tabs, axis=0)


def _conv_body(cur_ref, prev_ref, nxt_ref, w_ref, b_ref, xs_ref, bc_ref, *, nblk, width, d_inner):
    j = pl.program_id(0) % nblk
    pad = width // 2
    cur = cur_ref[...]
    prev = jnp.where(j == 0, 0.0, prev_ref[...])
    nxt = jnp.where(j == nblk - 1, 0.0, nxt_ref[...])
    ext = jnp.concatenate([prev, cur, nxt], axis=0)
    tm = cur.shape[0]
    acc = jnp.zeros_like(cur) + b_ref[...]
    for k in range(width):
        off = SUBLANES - pad + k
        acc = acc + ext[off:off + tm, :] * w_ref[k:k + 1, :]
    act = acc * jax.nn.sigmoid(acc)
    xs_ref[...] = act[:, :d_inner]
    bc_ref[...] = act[:, d_inner:].astype(BF16)


def _conv(xbc, w, b, seqlen, d_inner, tm=256):
    t, cdim = xbc.shape
    width = w.shape[0]
    nblk = seqlen // tm
    r = tm // SUBLANES
    last = t // SUBLANES - 1
    return pl.pallas_call(
        functools.partial(_conv_body, nblk=nblk, width=width, d_inner=d_inner),
        grid=(t // tm,),
        in_specs=[pl.BlockSpec((tm, cdim), lambda i: (i, 0)),
                  pl.BlockSpec((SUBLANES, cdim), lambda i: (jnp.maximum(i * r - 1, 0), 0)),
                  pl.BlockSpec((SUBLANES, cdim), lambda i: (jnp.minimum((i + 1) * r, last), 0)),
                  _full(w.shape), _full(b.shape)],
        out_specs=[pl.BlockSpec((tm, d_inner), lambda i: (i, 0)),
                   pl.BlockSpec((tm, cdim - d_inner), lambda i: (i, 0))],
        out_shape=[jax.ShapeDtypeStruct((t, d_inner), F32),
                   jax.ShapeDtypeStruct((t, cdim - d_inner), BF16)],
        compiler_params=_params(("parallel",)),
        name="ssd_conv",
    )(xbc, xbc, xbc, w, b)


def _cumsum_rows(x):
    n = x.shape[0]
    ri = lax.broadcasted_iota(jnp.int32, x.shape, 0)
    k = 1
    while k < n:
        x = x + jnp.where(ri >= k, pltpu.roll(x, k, 0), 0.0)
        k *= 2
    return x


def _softplus(x):
    return jnp.maximum(x, 0.0) + jnp.log1p(jnp.exp(-jnp.abs(x)))


def _ssd_body(xs_ref, bc_ref, dt_ref, bias_ref, a_ref, y_ref, st_ref, *, reverse, heads):
    @pl.when(pl.program_id(1) == 0)
    def _():
        st_ref[...] = jnp.zeros_like(st_ref)

    n = SSD_CHUNK
    p = SSD_HEADDIM
    ng = SSD_GROUPS
    ns = SSD_STATE
    hg = heads // ng
    dt = _softplus(dt_ref[...] + bias_ref[...])
    adt = dt * a_ref[...]
    cs = _cumsum_rows(adt)
    tot = cs[n - 1:n, :]
    if reverse:
        cs = tot - cs + adt
    cs_t = cs.T
    dt_t = dt.T
    ecs = jnp.exp(cs)
    dst = jnp.exp(tot - cs) * dt
    etot = jnp.exp(tot)
    li = lax.broadcasted_iota(jnp.int32, (n, n), 0)
    si = lax.broadcasted_iota(jnp.int32, (n, n), 1)
    mask = (si >= li) if reverse else (si <= li)
    for g in range(ng):
        bg = bc_ref[:, g * ns:(g + 1) * ns]
        cg = bc_ref[:, ng * ns + g * ns:ng * ns + (g + 1) * ns]
        cb = lax.dot_general(cg, bg, (((1,), (1,)), ((), ())), preferred_element_type=F32)
        for hh in range(hg):
            h = g * hg + hh
            xh = xs_ref[:, h * p:(h + 1) * p]
            lm = jnp.exp(jnp.where(mask, cs[:, h:h + 1] - cs_t[h:h + 1, :], -jnp.inf))
            m = (cb * lm * dt_t[h:h + 1, :]).astype(BF16)
            yd = _dot(m, xh.astype(BF16))
            sp = st_ref[h]
            yo = _dot(cg, sp.astype(BF16)) * ecs[:, h:h + 1]
            y_ref[:, h * p:(h + 1) * p] = yd + yo
            xw = (xh * dst[:, h:h + 1]).astype(BF16)
            stn = lax.dot_general(bg, xw, (((0,), (0,)), ((), ())), preferred_element_type=F32)
            st_ref[h] = sp * etot[:, h:h + 1] + stn


def _ssd_dir(xs, bc, dt, bias, a, bsz, seqlen, heads, reverse):
    t, d_inner = xs.shape
    n = SSD_CHUNK
    nc = seqlen // n

    def blk(b, j):
        return (b * nc + ((nc - 1 - j) if reverse else j), 0)

    return pl.pallas_call(
        functools.partial(_ssd_body, reverse=reverse, heads=heads),
        grid=(bsz, nc),
        in_specs=[pl.BlockSpec((n, d_inner), blk), pl.BlockSpec((n, bc.shape[1]), blk),
                  pl.BlockSpec((n, LANES), blk), _full(bias.shape), _full(a.shape)],
        out_specs=pl.BlockSpec((n, d_inner), blk),
        out_shape=jax.ShapeDtypeStruct((t, d_inner), F32),
        scratch_shapes=[pltpu.VMEM((heads, SSD_STATE, SSD_HEADDIM), F32)],
        compiler_params=_params(("parallel", "arbitrary")),
        name="ssd_bwd" if reverse else "ssd_fwd",
    )(xs, bc, dt, bias, a)


def _gelu(x):
    return 0.5 * x * (1.0 + lax.erf(x * (2.0 ** -0.5)))


def _merge_body(x_ref, u_ref, y5f_ref, y5b_ref, d5_ref, wglu_ref, ga_ref,
                xs_ref, ysf_ref, ysb_ref, dss_ref, z_ref, nw_ref, wso_ref, gb_ref, wo_ref, fnw_ref,
                h_ref, xn_ref, *, d_model, groups):
    ya = u_ref[...].astype(F32) * d5_ref[...] + y5f_ref[...] + y5b_ref[...]
    glu = _dot(_gelu(ya).astype(BF16), wglu_ref[...])
    ya = glu[:, :d_model] * jax.nn.sigmoid(glu[:, d_model:])
    z = z_ref[...]
    yb = (xs_ref[...] * dss_ref[...] + ysf_ref[...] + ysb_ref[...]) * (z * jax.nn.sigmoid(z))
    gw = yb.shape[1] // groups
    parts = []
    for g in range(groups):
        yg = yb[:, g * gw:(g + 1) * gw]
        parts.append(yg * lax.rsqrt(jnp.mean(yg * yg, axis=-1, keepdims=True) + RMS_EPS))
    yb = jnp.concatenate(parts, axis=1) * nw_ref[...]
    yb = _dot(yb.astype(BF16), wso_ref[...])
    merged = jax.nn.sigmoid(ga_ref[...]) * ya + jax.nn.sigmoid(gb_ref[...]) * yb
    h = x_ref[...] + _dot(merged.astype(BF16), wo_ref[...])
    h_ref[...] = h
    xn_ref[...] = _rms(h, fnw_ref[...])


def _merge(x2, u, y5f, y5b, d5, wglu, ga, xs, ysf, ysb, dss, z, nw, wso, gb, wo, fnw, tm=256):
    t, d = x2.shape
    args = [x2, u, y5f, y5b, d5, wglu, ga, xs, ysf, ysb, dss, z, nw, wso, gb, wo, fnw]
    per_token = [True, True, True, True, False, False, True, True, True, True, False, True, False, False, True,
                 False, False]
    specs = []
    for a, tok in zip(args, per_token):
        if tok:
            specs.append(pl.BlockSpec((tm, a.shape[1]), lambda i: (i, 0)))
        else:
            specs.append(_full(a.shape))
    return pl.pallas_call(
        functools.partial(_merge_body, d_model=d, groups=SSD_GROUPS),
        grid=(t // tm,),
        in_specs=specs,
        out_specs=[pl.BlockSpec((tm, d), lambda i: (i, 0))] * 2,
        out_shape=[jax.ShapeDtypeStruct((t, d), F32)] * 2,
        compiler_params=_params(("parallel",)),
        name="merge",
    )(*args)


def _make_mixer(mix_norm_w, w_in, s5_a_re, s5_a_im, s5_log_dt, s5_b_re, s5_b_im, s5_c_re, s5_c_im,
                s5_d, w_glu, conv_w, conv_b, ssd_dt_bias, ssd_a_log, ssd_d, ssd_norm_w, w_ssd_out, w_o, ffn_norm_w):
    d = w_in.shape[0]
    s5w = s5_d.shape[0]
    d_inner = w_ssd_out.shape[0]
    heads = ssd_d.shape[0]
    cdim = conv_b.shape[0]
    o1 = s5w
    o2 = o1 + d_inner
    o3 = o2 + cdim
    o4 = o3 + 2 * heads

    def padw(w):
        return jnp.pad(w, ((0, 0), (0, LANES - w.shape[1])))

    ws = [w_in[:, :o1], w_in[:, o1:o2], w_in[:, o2:o3], padw(w_in[:, o3:o3 + heads]),
          padw(w_in[:, o3 + heads:o4]), w_in[:, o4:o4 + d], w_in[:, o4 + d:]]
    ws = [w.astype(BF16) for w in ws]
    nw = mix_norm_w.reshape(1, d).astype(F32)
    s5p = [_s5_prep(s5_a_re[k], s5_a_im[k], s5_log_dt[k], s5_b_re[k], s5_b_im[k], s5_c_re[k], s5_c_im[k],
                    bool(k)) for k in (0, 1)]
    cw = conv_w.reshape(conv_w.shape[0], cdim).astype(F32)
    cb = conv_b.reshape(1, cdim).astype(F32)
    bias = [jnp.pad(ssd_dt_bias[k].astype(F32), (0, LANES - heads)).reshape(1, LANES) for k in (0, 1)]
    a_neg = [jnp.pad(-jnp.exp(ssd_a_log[k].astype(F32)), (0, LANES - heads)).reshape(1, LANES) for k in (0, 1)]
    dss = jnp.repeat(ssd_d.astype(F32), d_inner // heads).reshape(1, d_inner)
    d5 = s5_d.reshape(1, s5w).astype(F32)
    wglu = w_glu.astype(BF16)
    snw = ssd_norm_w.reshape(1, d_inner).astype(F32)
    wso = w_ssd_out.astype(BF16)
    wo = w_o.astype(BF16)
    fnw = ffn_norm_w.reshape(1, d).astype(F32)

    def run(x2, bsz, seqlen):
        u, z, xbc, dtf, dtb, ga, gb = _in_proj(x2, nw, ws)
        y5 = [_s5_dir(u, *s5p[k], bsz, seqlen, bool(k)) for k in (0, 1)]
        xs, bc = _conv(xbc, cw, cb, seqlen, d_inner)
        ys = [_ssd_dir(xs, bc, (dtf, dtb)[k], bias[k], a_neg[k], bsz, seqlen, heads, bool(k)) for k in (0, 1)]
        return _merge(x2, u, y5[0], y5[1], d5, wglu, ga, xs, ys[0], ys[1], dss, z, snw, wso, gb, wo, fnw)

    return run


def _split_bf16(x):
    hi = x.astype(BF16)
    return hi, (x - hi.astype(F32)).astype(BF16)


def _dot3(ah, al, bh, bl, dims=(((1,), (0,)), ((), ()))):
    def dg(p, q):
        return lax.dot_general(p, q, dims, preferred_element_type=F32)
    return dg(ah, bh) + (dg(al, bh) + dg(ah, bl))


def _topk_rows(xs, k, payloads=None):
    n = xs[0].shape[0]
    ri = lax.broadcasted_iota(jnp.int32, xs[0].shape, 0)
    xs = list(xs)
    vals = [[] for _ in xs]
    outs = [[] for _ in xs]
    for _ in range(k):
        for i, x in enumerate(xs):
            m = jnp.max(x, axis=0, keepdims=True)
            pos = jnp.min(jnp.where(x == m, ri, n), axis=0, keepdims=True)
            sel = ri == pos
            vals[i].append(m)
            if payloads is None:
                outs[i].append(pos)
            else:
                outs[i].append(jnp.max(jnp.where(sel, payloads[i], -1), axis=0, keepdims=True))
            xs[i] = jnp.where(sel, -jnp.inf, x)
    return [(jnp.concatenate(v, axis=0), jnp.concatenate(o, axis=0)) for v, o in zip(vals, outs)]


def _pair_candidates(c1, c2, combine):
    r = SUBLANES
    top = lax.broadcasted_iota(jnp.int32, (r, c1.shape[1]), 0) < r // 2
    lo = c2[0:r]
    quad = jnp.concatenate([c2[0:r // 2], c2[0:r // 2]], axis=0)
    blocks = [combine(c1[0:1], lo), combine(c1[0:1], c2[r:2 * r])]
    blocks += [combine(c1[a:a + 1], lo) for a in (1, 2, 3)]
    blocks += [combine(jnp.where(top, c1[a:a + 1], c1[a + 1:a + 2]), quad) for a in (4, 6)]
    blocks.append(combine(c1[r:2 * r], c2[0:1]))
    return jnp.concatenate(blocks, axis=0)


def _route_body(xn_ref, wqh_ref, wql_ref, kh_ref, kl_ref, e_ref, g_ref, q_ref, *, heads, nk, dk):
    k = PEER_TOPK
    xh, xl = _split_bf16(xn_ref[...])
    q_ref[...] = _dot3(xh, xl, wqh_ref[...], wql_ref[...])
    nt = (((1,), (1,)), ((), ()))

    def head_pair(hp, _):
        cands, ids = [], []
        for hh in range(2):
            h = hp * 2 + hh
            scores = []
            for s in range(2):
                col = pl.multiple_of((h * 2 + s) * dk, dk)
                qh, ql = _split_bf16(q_ref[:, pl.ds(col, dk)])
                scores.append(_dot3(kh_ref[h * 2 + s], kl_ref[h * 2 + s], qh, ql, nt))
            (s1, i1), (s2, i2) = _topk_rows(scores, k)
            cands.append(_pair_candidates(s1, s2, lambda a, b: a + b))
            ids.append(_pair_candidates(i1, i2, lambda a, b: a * nk + b))
        picks = _topk_rows(cands, k, ids)
        for hh, (best, expert) in enumerate(picks):
            w = jnp.exp(best - best[0:1, :])
            gate = w / jnp.sum(w, axis=0, keepdims=True)
            rows = pl.ds(pl.multiple_of((hp * 2 + hh) * k, k), k)
            e_ref[0, rows, :] = expert
            g_ref[0, rows, :] = gate
        return 0

    lax.fori_loop(0, heads // 2, head_pair, 0)


def _peer_route(xn, tok_base, ntok, wqh, wql, kh, kl, heads, nk, tm=LANES):
    d = xn.shape[1]
    dk = kh.shape[-1]
    nb = ntok // tm
    b0 = tok_base // tm
    slots = heads * PEER_TOPK
    return pl.pallas_call(
        functools.partial(_route_body, heads=heads, nk=nk, dk=dk),
        grid=(nb,),
        in_specs=[pl.BlockSpec((tm, d), lambda i: (i + b0, 0)), _full(wqh.shape), _full(wql.shape),
                  _full(kh.shape), _full(kl.shape)],
        out_specs=[pl.BlockSpec((1, slots, tm), lambda i: (i, 0, 0))] * 2,
        out_shape=[jax.ShapeDtypeStruct((nb, slots, tm), jnp.int32),
                   jax.ShapeDtypeStruct((nb, slots, tm), F32)],
        scratch_shapes=[pltpu.VMEM((tm, wqh.shape[1]), F32)],
        compiler_params=_params(("parallel",)),
        name="peer_route",
    )(xn, wqh, wql, kh, kl)


SC_CORES = 2
SC_SUBCORES = 16
SC_LANES = 16
ROW_UNIT = 32


def _sc_experts(uv_pack, idx, xn, tok_base, slots):
    from jax.experimental.pallas import tpu_sc as plsc
    r = SUBLANES
    n = idx.shape[0]
    d = xn.shape[1]
    half = d // 2
    workers = SC_CORES * SC_SUBCORES
    per = n // workers
    tile_rows = slots * r
    tiles = per // tile_rows
    upt = tile_rows // ROW_UNIT
    nl = SC_LANES
    idx3 = idx.reshape(workers, tiles * upt, ROW_UNIT)
    mesh = plsc.VectorSubcoreMesh(core_axis_name="c", subcore_axis_name="s")

    @functools.partial(
        pl.kernel, mesh=mesh,
        out_type=[jax.ShapeDtypeStruct((n,), F32), jax.ShapeDtypeStruct((n, half), jnp.uint32)],
        scratch_types=[pltpu.VMEM((tiles * upt, ROW_UNIT), jnp.int32),
                       pltpu.VMEM((ROW_UNIT, d), jnp.uint32), pltpu.VMEM((ROW_UNIT, d), jnp.uint32),
                       pltpu.VMEM((r, d), F32), pltpu.VMEM((r, d), F32),
                       pltpu.VMEM((per,), F32),
                       pltpu.SemaphoreType.DMA, pltpu.SemaphoreType.DMA,
                       pltpu.SemaphoreType.DMA, pltpu.SemaphoreType.DMA,
                       pltpu.SemaphoreType.DMA, pltpu.SemaphoreType.DMA],
        compiler_params=pltpu.CompilerParams(needs_layout_passes=False),
        name="peer_experts",
    )
    def experts(uv_hbm, idx_hbm, x_hbm, hid_hbm, vst_hbm, idx_v, rows_a, rows_b, x_a, x_b, hid_v,
                ga, gb, wa, wb, xa, xb):
        wid = lax.axis_index("s") * SC_CORES + lax.axis_index("c")
        row0 = wid * per
        tok0 = tok_base + wid * tiles * r
        pltpu.sync_copy(idx_hbm.at[wid], idx_v)
        bufs = ((rows_a, ga, wa), (rows_b, gb, wb))
        xbufs = ((x_a, xa), (x_b, xb))

        def rows_copy(unit, which):
            buf, sem, _ = bufs[which]
            return pltpu.make_async_copy(uv_hbm.at[idx_v.at[unit]], buf, sem)

        def v_copy(unit, which):
            buf, _, sem = bufs[which]
            return pltpu.make_async_copy(buf.at[:, pl.ds(half, half)],
                                         vst_hbm.at[pl.ds(row0 + unit * ROW_UNIT, ROW_UNIT)], sem)

        def x_copy(tile, which):
            buf, sem = xbufs[which]
            return pltpu.make_async_copy(x_hbm.at[pl.ds(tok0 + tile * r, r)], buf, sem)

        def compute(rows_ref, x_ref, unit):
            lane = lax.iota(jnp.int32, nl)
            for grp in range(ROW_UNIT // nl):
                def body(j, acc, grp=grp):
                    off = pl.multiple_of(j * nl, nl)
                    x_lo = [x_ref[tk, pl.ds(off, nl)] for tk in range(r)]
                    x_hi = [x_ref[tk, pl.ds(half + off, nl)] for tk in range(r)]
                    new = []
                    for i in range(nl):
                        w = rows_ref[grp * nl + i, pl.ds(off, nl)]
                        lo = lax.bitcast_convert_type(w << 16, F32)
                        hi = lax.bitcast_convert_type(w & jnp.uint32(0xFFFF0000), F32)
                        new.append(acc[i] + lo * x_lo[i % r] + hi * x_hi[i % r])
                    return tuple(new)

                acc = lax.fori_loop(0, half // nl, body, tuple(jnp.zeros((nl,), F32) for _ in range(nl)))
                res = jnp.zeros((nl,), F32)
                for i in range(nl):
                    res = jnp.where(lane == i, jnp.sum(acc[i]), res)
                tile = unit // upt
                p = (unit - tile * upt) * ROW_UNIT + grp * nl + lane
                dest = tile * tile_rows + (p & (r - 1)) * slots + (p >> (r.bit_length() - 1))
                plsc.store_scatter(hid_v, [dest], res)

        rows_copy(0, 0).start()
        x_copy(0, 0).start()

        @pl.loop(0, tiles, step=2)
        def _(tile0):
            for tt in range(2):
                tile = tile0 + tt
                x_copy(tile, tt).wait()

                @pl.when(tile + 1 < tiles)
                def _():
                    x_copy(tile + 1, 1 - tt).start()

                @pl.loop(0, upt, step=2)
                def _(q0):
                    for qq in range(2):
                        unit = tile * upt + q0 + qq
                        @pl.when(unit > 0)
                        def _():
                            v_copy(unit - 1, 1 - qq).wait()

                        @pl.when(unit + 1 < tiles * upt)
                        def _():
                            rows_copy(unit + 1, 1 - qq).start()

                        rows_copy(unit, qq).wait()
                        v_copy(unit, qq).start()
                        compute(bufs[qq][0], xbufs[tt][0], unit)

        v_copy(tiles * upt - 1, 1).wait()
        pltpu.sync_copy(hid_v, hid_hbm.at[pl.ds(row0, per)])

    return experts(uv_pack, idx3, xn)


EVAL_TILES = 8


def _pack_bf16_halves(tab):
    half = tab.shape[1] // 2
    bits = lax.bitcast_convert_type(tab.astype(BF16), jnp.uint16).astype(jnp.uint32)
    return bits[:, :half] | (bits[:, half:] << 16)


def _unpack_bf16_halves(w):
    lo = lax.bitcast_convert_type(w << 16, F32)
    hi = lax.bitcast_convert_type(w & jnp.uint32(0xFFFF0000), F32)
    return lo, hi


def _eval_body(vg_ref, hid_ref, g_ref, h_ref, fw_ref, o_ref, *, slots):
    r = SUBLANES
    half = h_ref.shape[1] // 2
    @pl.loop(0, EVAL_TILES)
    def _(tile):
        rows = pl.ds(pl.multiple_of(tile * r, r), r)
        act = _gelu(hid_ref[rows, :]) * g_ref[rows, :]
        acc_lo = jnp.zeros((r, half), F32)
        acc_hi = jnp.zeros((r, half), F32)
        for k in range(slots):
            v_lo, v_hi = _unpack_bf16_halves(vg_ref[tile, k * r:(k + 1) * r, :])
            a = act[:, k:k + 1]
            acc_lo = acc_lo + a * v_lo
            acc_hi = acc_hi + a * v_hi
        o_ref[rows, :] = _rms(h_ref[rows, :] + jnp.concatenate([acc_lo, acc_hi], axis=1), fw_ref[...])


def _peer_eval(vg, hid, gates, h, tok_base, fw, slots):
    t = hid.shape[0]
    d = h.shape[1]
    r = SUBLANES
    g = EVAL_TILES
    s0 = tok_base // (r * g)
    vg = vg.reshape(t // r, slots * r, d // 2)
    return pl.pallas_call(
        functools.partial(_eval_body, slots=slots),
        grid=(t // (r * g),),
        in_specs=[pl.BlockSpec((g, slots * r, d // 2), lambda i: (i, 0, 0)),
                  pl.BlockSpec((g * r, slots), lambda i: (i, 0)), pl.BlockSpec((g * r, slots), lambda i: (i, 0)),
                  pl.BlockSpec((g * r, d), lambda i: (i + s0, 0)), _full(fw.shape)],
        out_specs=pl.BlockSpec((g * r, d), lambda i: (i, 0)),
        out_shape=jax.ShapeDtypeStruct((t, d), F32),
        compiler_params=_params(("parallel",)),
        name="peer_eval",
    )(vg, hid, gates, h, fw)


PEER_CHUNK_TOKENS = 2048
PEER_CHUNKS_AHEAD = 12


def _make_peer(w_q, sub_keys, u_tab, v_tab, final_norm_w):
    d = w_q.shape[0]
    heads, _, nk, dk = sub_keys.shape
    slots = heads * PEER_TOPK
    wqh, wql = _split_bf16(w_q.astype(F32))
    kh, kl = _split_bf16(sub_keys.reshape(heads * 2, nk, dk).astype(F32))
    fw = final_norm_w.reshape(1, d).astype(F32)
    uv_pack = jnp.concatenate([_pack_bf16_halves(u_tab.astype(F32)), _pack_bf16_halves(v_tab.astype(F32))],
                              axis=1)
    r = SUBLANES

    def chunk(xn, h, tok_base, ntok, after):
        expert_t, gates_t = _peer_route(xn, tok_base, ntok, wqh, wql, kh, kl, heads, nk)
        nb, _, tm = expert_t.shape
        idx = expert_t.reshape(nb, slots, tm // r, r).transpose(0, 2, 1, 3).reshape(-1)
        gates = gates_t.transpose(0, 2, 1).reshape(ntok, slots)
        if after is not None:
            idx, after = lax.optimization_barrier((idx, after))
        hid, vg = _sc_experts(uv_pack, idx, xn, tok_base, slots)
        return _peer_eval(vg, hid.reshape(ntok, slots), gates, h, tok_base, fw, slots), after

    return chunk


def kernel(x, mix_norm_w, w_in, s5_a_re, s5_a_im, s5_log_dt, s5_b_re, s5_b_im, s5_c_re, s5_c_im, s5_d, w_glu, conv_w, conv_b, ssd_dt_bias, ssd_a_log, ssd_d, ssd_norm_w, w_ssd_out, w_o, ffn_norm_w, peer_w_q, peer_sub_keys, peer_u, peer_v, final_norm_w):
    bsz, seqlen, d = x.shape
    mixer = _make_mixer(mix_norm_w[0], w_in[0], s5_a_re[0], s5_a_im[0], s5_log_dt[0], s5_b_re[0], s5_b_im[0],
                        s5_c_re[0], s5_c_im[0], s5_d[0], w_glu[0], conv_w[0], conv_b[0], ssd_dt_bias[0],
                        ssd_a_log[0], ssd_d[0], ssd_norm_w[0], w_ssd_out[0], w_o[0], ffn_norm_w[0])
    peer_chunk = _make_peer(peer_w_q[0], peer_sub_keys[0], peer_u[0], peer_v[0], final_norm_w)
    tc = min(PEER_CHUNK_TOKENS, seqlen)
    outs = []
    for b in range(bsz):
        h, xn = mixer(x[b], 1, seqlen)
        for c in range(seqlen // tc):
            k = len(outs) - PEER_CHUNKS_AHEAD
            out, done = peer_chunk(xn, h, c * tc, tc, outs[k] if k >= 0 else None)
            if k >= 0:
                outs[k] = done
            outs.append(out)
    return jnp.concatenate(outs, axis=0).reshape(bsz, seqlen, d).astype(x.dtype)
```

```python
import functools

import jax
import jax.numpy as jnp
from jax import lax
from jax.experimental import pallas as pl
from jax.experimental.pallas import tpu as pltpu

F32 = jnp.float32
BF16 = jnp.bfloat16

RMS_EPS = 1e-6
SUBLANES = 8
LANES = 128
SSD_CHUNK = 128
SSD_HEADDIM = 64
SSD_GROUPS = 4
SSD_STATE = 64
PEER_TOPK = 16
VMEM_LIMIT = 56 * 1024 * 1024


def _params(sem):
    return pltpu.CompilerParams(dimension_semantics=sem, vmem_limit_bytes=VMEM_LIMIT)


def _rms(x, w):
    return x * lax.rsqrt(jnp.mean(x * x, axis=-1, keepdims=True) + RMS_EPS) * w


def _dot(a, b):
    return jnp.dot(a, b, preferred_element_type=F32)


def _full(shape):
    n = len(shape)
    return pl.BlockSpec(shape, lambda *_: (0,) * n)


def _inproj_body(x_ref, nw_ref, wu_ref, wz_ref, wx_ref, wdf_ref, wdb_ref, wga_ref, wgb_ref,
                 u_ref, z_ref, xbc_ref, dtf_ref, dtb_ref, ga_ref, gb_ref):
    xb = _rms(x_ref[...], nw_ref[...]).astype(BF16)
    u_ref[...] = _dot(xb, wu_ref[...]).astype(BF16)
    z_ref[...] = _dot(xb, wz_ref[...])
    xbc_ref[...] = _dot(xb, wx_ref[...])
    dtf_ref[...] = _dot(xb, wdf_ref[...])
    dtb_ref[...] = _dot(xb, wdb_ref[...])
    ga_ref[...] = _dot(xb, wga_ref[...])
    gb_ref[...] = _dot(xb, wgb_ref[...])


def _in_proj(x2, nw, ws, tm=512):
    t, d = x2.shape
    widths = [w.shape[1] for w in ws]
    dts = [BF16] + [F32] * 6
    return pl.pallas_call(
        _inproj_body,
        grid=(t // tm,),
        in_specs=[pl.BlockSpec((tm, d), lambda i: (i, 0)), _full((1, d))] + [_full(w.shape) for w in ws],
        out_specs=[pl.BlockSpec((tm, n), lambda i: (i, 0)) for n in widths],
        out_shape=[jax.ShapeDtypeStruct((t, n), dt) for n, dt in zip(widths, dts)],
        compiler_params=_params(("parallel",)),
        name="in_proj",
    )(x2, nw, *ws)


S5_LANE_GROUP = 8


def _s5_body(u_ref, bd_ref, cd_ref, tab_ref, y_ref, s_ref, car_ref, *, reverse, lb, ns):
    @pl.when(pl.program_id(1) == 0)
    def _():
        car_ref[...] = jnp.zeros_like(car_ref)

    w_in = u_ref.shape[1]
    ratio = ns // w_in
    slab = LANES
    for part in range(2):
        for c0 in range(0, ns, slab * ratio // 2):
            k0 = (c0 // ratio) // slab * slab
            cols = slice(part * ns + c0, part * ns + c0 + slab * ratio // 2)
            s_ref[:, cols] = _dot(u_ref[:, k0:k0 + slab], bd_ref[k0:k0 + slab, cols])
    nt = lb // SUBLANES
    edge = 0 if reverse else SUBLANES - 1

    for c0 in range(0, ns // LANES, S5_LANE_GROUP):
        los = [(c0 + g) * LANES for g in range(S5_LANE_GROUP)]

        def tile_step(i, carry, los=los):
            r = (nt - 1 - i) if reverse else i
            rows = pl.ds(pl.multiple_of(r * SUBLANES, SUBLANES), SUBLANES)
            new = []
            for g, lo in enumerate(los):
                re = s_ref[rows, lo:lo + LANES]
                im = s_ref[rows, ns + lo:ns + lo + LANES]
                for kk, k in enumerate((1, 2, 4)):
                    pr = tab_ref[2 * kk, :, lo:lo + LANES]
                    pi = tab_ref[2 * kk + 1, :, lo:lo + LANES]
                    sh = (SUBLANES - k) if reverse else k
                    rr = pltpu.roll(re, sh, 0)
                    ri = pltpu.roll(im, sh, 0)
                    re, im = re + pr * rr - pi * ri, im + pr * ri + pi * rr
                pr = tab_ref[6, :, lo:lo + LANES]
                pi = tab_ref[7, :, lo:lo + LANES]
                cr, ci = carry[2 * g], carry[2 * g + 1]
                re, im = re + pr * cr - pi * ci, im + pr * ci + pi * cr
                s_ref[rows, lo:lo + LANES] = re
                s_ref[rows, ns + lo:ns + lo + LANES] = im
                new.append(jnp.broadcast_to(re[edge:edge + 1, :], (SUBLANES, LANES)))
                new.append(jnp.broadcast_to(im[edge:edge + 1, :], (SUBLANES, LANES)))
            return tuple(new)

        carry0 = []
        for lo in los:
            carry0.append(car_ref[0, :, lo:lo + LANES])
            carry0.append(car_ref[1, :, lo:lo + LANES])
        carry = lax.fori_loop(0, nt, tile_step, tuple(carry0))
        for g, lo in enumerate(los):
            car_ref[0, :, lo:lo + LANES] = carry[2 * g]
            car_ref[1, :, lo:lo + LANES] = carry[2 * g + 1]

    ctile = 2 * LANES
    for c0 in range(0, w_in, ctile):
        acc = None
        for part in range(2):
            rows = slice(part * ns + c0 * ratio, part * ns + (c0 + ctile) * ratio)
            term = _dot(s_ref[:, rows].astype(BF16), cd_ref[rows, c0:c0 + ctile])
            acc = term if acc is None else acc + term
        y_ref[:, c0:c0 + ctile] = acc


def _s5_dir(u, bd, cd, tab, bsz, seqlen, reverse, lb=256):
    t, w = u.shape
    ns = bd.shape[1] // 2
    nb = seqlen // lb

    def blk(b, j):
        return (b * nb + ((nb - 1 - j) if reverse else j), 0)

    return pl.pallas_call(
        functools.partial(_s5_body, reverse=reverse, lb=lb, ns=ns),
        grid=(bsz, nb),
        in_specs=[pl.BlockSpec((lb, w), blk), _full(bd.shape), _full(cd.shape), _full(tab.shape)],
        out_specs=pl.BlockSpec((lb, w), blk),
        out_shape=jax.ShapeDtypeStruct((t, w), F32),
        scratch_shapes=[pltpu.VMEM((lb, 2 * ns), F32), pltpu.VMEM((2, SUBLANES, ns), F32)],
        compiler_params=_params(("parallel", "arbitrary")),
        name="s5_bwd" if reverse else "s5_fwd",
    )(u, bd, cd, tab)


def _s5_prep(a_re, a_im, log_dt, b_re, b_im, c_re, c_im, reverse):
    g, p = a_re.shape
    c = b_re.shape[-1]
    ar = a_re.astype(F32)
    ai = a_im.astype(F32)
    step = jnp.exp(log_dt.astype(F32))[:, None]
    mag = jnp.exp(ar * step)
    lam_r = mag * jnp.cos(ai * step)
    lam_i = mag * jnp.sin(ai * step)
    den = ar * ar + ai * ai
    f_r = ((lam_r - 1.0) * ar + lam_i * ai) / den
    f_i = (lam_i * ar - (lam_r - 1.0) * ai) / den
    bt_r = f_r[..., None] * b_re - f_i[..., None] * b_im
    bt_i = f_r[..., None] * b_im + f_i[..., None] * b_re
    eye = jnp.eye(g, dtype=F32)
    bd = jnp.concatenate([jnp.einsum('gpc,gh->gchp', bt_r, eye).reshape(g * c, g * p),
                          jnp.einsum('gpc,gh->gchp', bt_i, eye).reshape(g * c, g * p)], axis=1)
    cd = jnp.concatenate([jnp.einsum('gcp,gh->gphc', c_re, eye).reshape(g * p, g * c),
                          -jnp.einsum('gcp,gh->gphc', c_im, eye).reshape(g * p, g * c)], axis=0)
    lr = lam_r.reshape(1, g * p)
    li = lam_i.reshape(1, g * p)
    pw = [(lr, li)]
    for _ in range(SUBLANES - 1):
        qr, qi = pw[-1]
        pw.append((qr * lr - qi * li, qr * li + qi * lr))
    row = jnp.arange(SUBLANES)[:, None]
    tabs = []
    for k in (1, 2, 4):
        keep = (row < SUBLANES - k) if reverse else (row >= k)
        for comp in pw[k - 1]:
            tabs.append(jnp.where(keep, comp, 0.0))
    order = list(range(SUBLANES - 1, -1, -1)) if reverse else list(range(SUBLANES))
    for comp in (0, 1):
        tabs.append(jnp.concatenate([pw[k][comp] for k in order], axis=0))
    return bd.astype(BF16), cd.astype(BF16), jnp.stack(tabs, axis=0)


def _conv_body(cur_ref, prev_ref, nxt_ref, w_ref, b_ref, xs_ref, bc_ref, *, nblk, width, d_inner):
    j = pl.program_id(0) % nblk
    pad = width // 2
    cur = cur_ref[...]
    prev = jnp.where(j == 0, 0.0, prev_ref[...])
    nxt = jnp.where(j == nblk - 1, 0.0, nxt_ref[...])
    ext = jnp.concatenate([prev, cur, nxt], axis=0)
    tm = cur.shape[0]
    acc = jnp.zeros_like(cur) + b_ref[...]
    for k in range(width):
        off = SUBLANES - pad + k
        acc = acc + ext[off:off + tm, :] * w_ref[k:k + 1, :]
    act = acc * jax.nn.sigmoid(acc)
    xs_ref[...] = act[:, :d_inner]
    bc_ref[...] = act[:, d_inner:].astype(BF16)


def _conv(xbc, w, b, seqlen, d_inner, tm=256):
    t, cdim = xbc.shape
    width = w.shape[0]
    nblk = seqlen // tm
    r = tm // SUBLANES
    last = t // SUBLANES - 1
    return pl.pallas_call(
        functools.partial(_conv_body, nblk=nblk, width=width, d_inner=d_inner),
        grid=(t // tm,),
        in_specs=[pl.BlockSpec((tm, cdim), lambda i: (i, 0)),
                  pl.BlockSpec((SUBLANES, cdim), lambda i: (jnp.maximum(i * r - 1, 0), 0)),
                  pl.BlockSpec((SUBLANES, cdim), lambda i: (jnp.minimum((i + 1) * r, last), 0)),
                  _full(w.shape), _full(b.shape)],
        out_specs=[pl.BlockSpec((tm, d_inner), lambda i: (i, 0)),
                   pl.BlockSpec((tm, cdim - d_inner), lambda i: (i, 0))],
        out_shape=[jax.ShapeDtypeStruct((t, d_inner), F32),
                   jax.ShapeDtypeStruct((t, cdim - d_inner), BF16)],
        compiler_params=_params(("parallel",)),
        name="ssd_conv",
    )(xbc, xbc, xbc, w, b)


def _cumsum_rows(x):
    n = x.shape[0]
    ri = lax.broadcasted_iota(jnp.int32, x.shape, 0)
    k = 1
    while k < n:
        x = x + jnp.where(ri >= k, pltpu.roll(x, k, 0), 0.0)
        k *= 2
    return x


def _softplus(x):
    return jnp.maximum(x, 0.0) + jnp.log1p(jnp.exp(-jnp.abs(x)))


def _ssd_body(xs_ref, bc_ref, dt_ref, bias_ref, a_ref, y_ref, st_ref, *, reverse, heads):
    @pl.when(pl.program_id(1) == 0)
    def _():
        st_ref[...] = jnp.zeros_like(st_ref)

    n = SSD_CHUNK
    p = SSD_HEADDIM
    ng = SSD_GROUPS
    ns = SSD_STATE
    hg = heads // ng
    dt = _softplus(dt_ref[...] + bias_ref[...])
    adt = dt * a_ref[...]
    cs = _cumsum_rows(adt)
    tot = cs[n - 1:n, :]
    if reverse:
        cs = tot - cs + adt
    cs_t = cs.T
    dt_t = dt.T
    ecs = jnp.exp(cs)
    dst = jnp.exp(tot - cs) * dt
    etot = jnp.exp(tot)
    li = lax.broadcasted_iota(jnp.int32, (n, n), 0)
    si = lax.broadcasted_iota(jnp.int32, (n, n), 1)
    mask = (si >= li) if reverse else (si <= li)
    for g in range(ng):
        bg = bc_ref[:, g * ns:(g + 1) * ns]
        cg = bc_ref[:, ng * ns + g * ns:ng * ns + (g + 1) * ns]
        cb = lax.dot_general(cg, bg, (((1,), (1,)), ((), ())), preferred_element_type=F32)
        for hh in range(hg):
            h = g * hg + hh
            xh = xs_ref[:, h * p:(h + 1) * p]
            lm = jnp.exp(jnp.where(mask, cs[:, h:h + 1] - cs_t[h:h + 1, :], -jnp.inf))
            m = (cb * lm * dt_t[h:h + 1, :]).astype(BF16)
            yd = _dot(m, xh.astype(BF16))
            sp = st_ref[h]
            yo = _dot(cg, sp.astype(BF16)) * ecs[:, h:h + 1]
            y_ref[:, h * p:(h + 1) * p] = yd + yo
            xw = (xh * dst[:, h:h + 1]).astype(BF16)
            stn = lax.dot_general(bg, xw, (((0,), (0,)), ((), ())), preferred_element_type=F32)
            st_ref[h] = sp * etot[:, h:h + 1] + stn


def _ssd_dir(xs, bc, dt, bias, a, bsz, seqlen, heads, reverse):
    t, d_inner = xs.shape
    n = SSD_CHUNK
    nc = seqlen // n

    def blk(b, j):
        return (b * nc + ((nc - 1 - j) if reverse else j), 0)

    return pl.pallas_call(
        functools.partial(_ssd_body, reverse=reverse, heads=heads),
        grid=(bsz, nc),
        in_specs=[pl.BlockSpec((n, d_inner), blk), pl.BlockSpec((n, bc.shape[1]), blk),
                  pl.BlockSpec((n, LANES), blk), _full(bias.shape), _full(a.shape)],
        out_specs=pl.BlockSpec((n, d_inner), blk),
        out_shape=jax.ShapeDtypeStruct((t, d_inner), F32),
        scratch_shapes=[pltpu.VMEM((heads, SSD_STATE, SSD_HEADDIM), F32)],
        compiler_params=_params(("parallel", "arbitrary")),
        name="ssd_bwd" if reverse else "ssd_fwd",
    )(xs, bc, dt, bias, a)


def _gelu(x):
    return 0.5 * x * (1.0 + lax.erf(x * (2.0 ** -0.5)))


def _merge_body(x_ref, u_ref, y5f_ref, y5b_ref, d5_ref, wglu_ref, ga_ref,
                xs_ref, ysf_ref, ysb_ref, dss_ref, z_ref, nw_ref, wso_ref, gb_ref, wo_ref, fnw_ref,
                h_ref, xn_ref, *, d_model, groups):
    ya = u_ref[...].astype(F32) * d5_ref[...] + y5f_ref[...] + y5b_ref[...]
    glu = _dot(_gelu(ya).astype(BF16), wglu_ref[...])
    ya = glu[:, :d_model] * jax.nn.sigmoid(glu[:, d_model:])
    z = z_ref[...]
    yb = (xs_ref[...] * dss_ref[...] + ysf_ref[...] + ysb_ref[...]) * (z * jax.nn.sigmoid(z))
    gw = yb.shape[1] // groups
    parts = []
    for g in range(groups):
        yg = yb[:, g * gw:(g + 1) * gw]
        parts.append(yg * lax.rsqrt(jnp.mean(yg * yg, axis=-1, keepdims=True) + RMS_EPS))
    yb = jnp.concatenate(parts, axis=1) * nw_ref[...]
    yb = _dot(yb.astype(BF16), wso_ref[...])
    merged = jax.nn.sigmoid(ga_ref[...]) * ya + jax.nn.sigmoid(gb_ref[...]) * yb
    h = x_ref[...] + _dot(merged.astype(BF16), wo_ref[...])
    h_ref[...] = h
    xn_ref[...] = _rms(h, fnw_ref[...])


def _merge(x2, u, y5f, y5b, d5, wglu, ga, xs, ysf, ysb, dss, z, nw, wso, gb, wo, fnw, tm=256):
    t, d = x2.shape
    args = [x2, u, y5f, y5b, d5, wglu, ga, xs, ysf, ysb, dss, z, nw, wso, gb, wo, fnw]
    per_token = [True, True, True, True, False, False, True, True, True, True, False, True, False, False, True,
                 False, False]
    specs = []
    for a, tok in zip(args, per_token):
        if tok:
            specs.append(pl.BlockSpec((tm, a.shape[1]), lambda i: (i, 0)))
        else:
            specs.append(_full(a.shape))
    return pl.pallas_call(
        functools.partial(_merge_body, d_model=d, groups=SSD_GROUPS),
        grid=(t // tm,),
        in_specs=specs,
        out_specs=[pl.BlockSpec((tm, d), lambda i: (i, 0))] * 2,
        out_shape=[jax.ShapeDtypeStruct((t, d), F32)] * 2,
        compiler_params=_params(("parallel",)),
        name="merge",
    )(*args)


def _make_mixer(mix_norm_w, w_in, s5_a_re, s5_a_im, s5_log_dt, s5_b_re, s5_b_im, s5_c_re, s5_c_im,
                s5_d, w_glu, conv_w, conv_b, ssd_dt_bias, ssd_a_log, ssd_d, ssd_norm_w, w_ssd_out, w_o, ffn_norm_w):
    d = w_in.shape[0]
    s5w = s5_d.shape[0]
    d_inner = w_ssd_out.shape[0]
    heads = ssd_d.shape[0]
    cdim = conv_b.shape[0]
    o1 = s5w
    o2 = o1 + d_inner
    o3 = o2 + cdim
    o4 = o3 + 2 * heads

    def padw(w):
        return jnp.pad(w, ((0, 0), (0, LANES - w.shape[1])))

    ws = [w_in[:, :o1], w_in[:, o1:o2], w_in[:, o2:o3], padw(w_in[:, o3:o3 + heads]),
          padw(w_in[:, o3 + heads:o4]), w_in[:, o4:o4 + d], w_in[:, o4 + d:]]
    ws = [w.astype(BF16) for w in ws]
    nw = mix_norm_w.reshape(1, d).astype(F32)
    s5p = [_s5_prep(s5_a_re[k], s5_a_im[k], s5_log_dt[k], s5_b_re[k], s5_b_im[k], s5_c_re[k], s5_c_im[k],
                    bool(k)) for k in (0, 1)]
    cw = conv_w.reshape(conv_w.shape[0], cdim).astype(F32)
    cb = conv_b.reshape(1, cdim).astype(F32)
    bias = [jnp.pad(ssd_dt_bias[k].astype(F32), (0, LANES - heads)).reshape(1, LANES) for k in (0, 1)]
    a_neg = [jnp.pad(-jnp.exp(ssd_a_log[k].astype(F32)), (0, LANES - heads)).reshape(1, LANES) for k in (0, 1)]
    dss = jnp.repeat(ssd_d.astype(F32), d_inner // heads).reshape(1, d_inner)
    d5 = s5_d.reshape(1, s5w).astype(F32)
    wglu = w_glu.astype(BF16)
    snw = ssd_norm_w.reshape(1, d_inner).astype(F32)
    wso = w_ssd_out.astype(BF16)
    wo = w_o.astype(BF16)
    fnw = ffn_norm_w.reshape(1, d).astype(F32)

    def run(x2, bsz, seqlen):
        u, z, xbc, dtf, dtb, ga, gb = _in_proj(x2, nw, ws)
        y5 = [_s5_dir(u, *s5p[k], bsz, seqlen, bool(k)) for k in (0, 1)]
        xs, bc = _conv(xbc, cw, cb, seqlen, d_inner)
        ys = [_ssd_dir(xs, bc, (dtf, dtb)[k], bias[k], a_neg[k], bsz, seqlen, heads, bool(k)) for k in (0, 1)]
        return _merge(x2, u, y5[0], y5[1], d5, wglu, ga, xs, ys[0], ys[1], dss, z, snw, wso, gb, wo, fnw)

    return run


def _split_bf16(x):
    hi = x.astype(BF16)
    return hi, (x - hi.astype(F32)).astype(BF16)


def _dot3(ah, al, bh, bl, dims=(((1,), (0,)), ((), ()))):
    def dg(p, q):
        return lax.dot_general(p, q, dims, preferred_element_type=F32)
    return dg(ah, bh) + (dg(al, bh) + dg(ah, bl))


def _topk_rows(xs, k, payloads=None):
    n = xs[0].shape[0]
    ri = lax.broadcasted_iota(jnp.int32, xs[0].shape, 0)
    xs = list(xs)
    vals = [[] for _ in xs]
    outs = [[] for _ in xs]
    for _ in range(k):
        for i, x in enumerate(xs):
            m = jnp.max(x, axis=0, keepdims=True)
            pos = jnp.min(jnp.where(x == m, ri, n), axis=0, keepdims=True)
            sel = ri == pos
            vals[i].append(m)
            if payloads is None:
                outs[i].append(pos)
            else:
                outs[i].append(jnp.max(jnp.where(sel, payloads[i], -1), axis=0, keepdims=True))
            xs[i] = jnp.where(sel, -jnp.inf, x)
    return [(jnp.concatenate(v, axis=0), jnp.concatenate(o, axis=0)) for v, o in zip(vals, outs)]


def _pair_candidates(c1, c2, combine):
    r = SUBLANES
    top = lax.broadcasted_iota(jnp.int32, (r, c1.shape[1]), 0) < r // 2
    lo = c2[0:r]
    quad = jnp.concatenate([c2[0:r // 2], c2[0:r // 2]], axis=0)
    blocks = [combine(c1[0:1], lo), combine(c1[0:1], c2[r:2 * r])]
    blocks += [combine(c1[a:a + 1], lo) for a in (1, 2, 3)]
    blocks += [combine(jnp.where(top, c1[a:a + 1], c1[a + 1:a + 2]), quad) for a in (4, 6)]
    blocks.append(combine(c1[r:2 * r], c2[0:1]))
    return jnp.concatenate(blocks, axis=0)


def _route_body(xn_ref, wqh_ref, wql_ref, kh_ref, kl_ref, e_ref, g_ref, q_ref, gt_ref, *, heads, nk, dk):
    k = PEER_TOPK
    xh, xl = _split_bf16(xn_ref[...])
    q_ref[...] = _dot3(xh, xl, wqh_ref[...], wql_ref[...])
    nt = (((1,), (1,)), ((), ()))

    def head_pair(hp, _):
        cands, ids = [], []
        for hh in range(2):
            h = hp * 2 + hh
            scores = []
            for s in range(2):
                col = pl.multiple_of((h * 2 + s) * dk, dk)
                qh, ql = _split_bf16(q_ref[:, pl.ds(col, dk)])
                scores.append(_dot3(kh_ref[h * 2 + s], kl_ref[h * 2 + s], qh, ql, nt))
            (s1, i1), (s2, i2) = _topk_rows(scores, k)
            cands.append(_pair_candidates(s1, s2, lambda a, b: a + b))
            ids.append(_pair_candidates(i1, i2, lambda a, b: a * nk + b))
        picks = _topk_rows(cands, k, ids)
        for hh, (best, expert) in enumerate(picks):
            w = jnp.exp(best - best[0:1, :])
            gate = w / jnp.sum(w, axis=0, keepdims=True)
            rows = pl.ds(pl.multiple_of((hp * 2 + hh) * k, k), k)
            e_ref[0, rows, :] = expert
            gt_ref[rows, :] = gate
        return 0

    lax.fori_loop(0, heads // 2, head_pair, 0)
    g_ref[...] = gt_ref[...].T


def _peer_route(xn, tok_base, ntok, wqh, wql, kh, kl, heads, nk, tm=LANES):
    d = xn.shape[1]
    dk = kh.shape[-1]
    nb = ntok // tm
    b0 = tok_base // tm
    slots = heads * PEER_TOPK
    return pl.pallas_call(
        functools.partial(_route_body, heads=heads, nk=nk, dk=dk),
        grid=(nb,),
        in_specs=[pl.BlockSpec((tm, d), lambda i: (i + b0, 0)), _full(wqh.shape), _full(wql.shape),
                  _full(kh.shape), _full(kl.shape)],
        out_specs=[pl.BlockSpec((1, slots, tm), lambda i: (i, 0, 0)), pl.BlockSpec((tm, slots), lambda i: (i, 0))],
        out_shape=[jax.ShapeDtypeStruct((nb, slots, tm), jnp.int32),
                   jax.ShapeDtypeStruct((ntok, slots), F32)],
        scratch_shapes=[pltpu.VMEM((tm, wqh.shape[1]), F32), pltpu.VMEM((slots, tm), F32)],
        compiler_params=_params(("parallel",)),
        name="peer_route",
    )(xn, wqh, wql, kh, kl)


SC_CORES = 2
SC_SUBCORES = 16
SC_LANES = 16
ROW_UNIT = 32


def _sc_experts(uv_pack, idx, xn, tok_base, slots):
    from jax.experimental.pallas import tpu_sc as plsc
    r = SUBLANES
    n = idx.shape[0]
    d = xn.shape[1]
    half = d // 2
    workers = SC_CORES * SC_SUBCORES
    per = n // workers
    tile_rows = slots * r
    tiles = per // tile_rows
    upt = tile_rows // ROW_UNIT
    nl = SC_LANES
    idx3 = idx.reshape(workers, tiles * upt, ROW_UNIT)
    mesh = plsc.VectorSubcoreMesh(core_axis_name="c", subcore_axis_name="s")

    @functools.partial(
        pl.kernel, mesh=mesh,
        out_type=[jax.ShapeDtypeStruct((n,), F32), jax.ShapeDtypeStruct((n, half), jnp.uint32)],
        scratch_types=[pltpu.VMEM((tiles * upt, ROW_UNIT), jnp.int32),
                       pltpu.VMEM((ROW_UNIT, d), jnp.uint32), pltpu.VMEM((ROW_UNIT, d), jnp.uint32),
                       pltpu.VMEM((r, d), F32), pltpu.VMEM((r, d), F32),
                       pltpu.VMEM((per,), F32),
                       pltpu.SemaphoreType.DMA, pltpu.SemaphoreType.DMA,
                       pltpu.SemaphoreType.DMA, pltpu.SemaphoreType.DMA,
                       pltpu.SemaphoreType.DMA, pltpu.SemaphoreType.DMA],
        compiler_params=pltpu.CompilerParams(needs_layout_passes=False),
        name="peer_experts",
    )
    def experts(uv_hbm, idx_hbm, x_hbm, hid_hbm, vst_hbm, idx_v, rows_a, rows_b, x_a, x_b, hid_v,
                ga, gb, wa, wb, xa, xb):
        wid = lax.axis_index("s") * SC_CORES + lax.axis_index("c")
        row0 = wid * per
        tok0 = tok_base + wid * tiles * r
        pltpu.sync_copy(idx_hbm.at[wid], idx_v)
        bufs = ((rows_a, ga, wa), (rows_b, gb, wb))
        xbufs = ((x_a, xa), (x_b, xb))

        def rows_copy(unit, which):
            buf, sem, _ = bufs[which]
            return pltpu.make_async_copy(uv_hbm.at[idx_v.at[unit]], buf, sem)

        def v_copy(unit, which):
            buf, _, sem = bufs[which]
            return pltpu.make_async_copy(buf.at[:, pl.ds(half, half)],
                                         vst_hbm.at[pl.ds(row0 + unit * ROW_UNIT, ROW_UNIT)], sem)

        def x_copy(tile, which):
            buf, sem = xbufs[which]
            return pltpu.make_async_copy(x_hbm.at[pl.ds(tok0 + tile * r, r)], buf, sem)

        def compute(rows_ref, x_ref, unit):
            lane = lax.iota(jnp.int32, nl)
            for grp in range(ROW_UNIT // nl):
                def body(j, acc, grp=grp):
                    off = pl.multiple_of(j * nl, nl)
                    x_lo = [x_ref[tk, pl.ds(off, nl)] for tk in range(r)]
                    x_hi = [x_ref[tk, pl.ds(half + off, nl)] for tk in range(r)]
                    new = []
                    for i in range(nl):
                        w = rows_ref[grp * nl + i, pl.ds(off, nl)]
                        lo = lax.bitcast_convert_type(w << 16, F32)
                        hi = lax.bitcast_convert_type(w & jnp.uint32(0xFFFF0000), F32)
                        new.append(acc[i] + lo * x_lo[i % r] + hi * x_hi[i % r])
                    return tuple(new)

                acc = lax.fori_loop(0, half // nl, body, tuple(jnp.zeros((nl,), F32) for _ in range(nl)))
                res = jnp.zeros((nl,), F32)
                for i in range(nl):
                    res = jnp.where(lane == i, jnp.sum(acc[i]), res)
                tile = unit // upt
                p = (unit - tile * upt) * ROW_UNIT + grp * nl + lane
                dest = tile * tile_rows + (p & (r - 1)) * slots + (p >> (r.bit_length() - 1))
                plsc.store_scatter(hid_v, [dest], res)

        rows_copy(0, 0).start()
        x_copy(0, 0).start()

        @pl.loop(0, tiles, step=2)
        def _(tile0):
            for tt in range(2):
                tile = tile0 + tt
                x_copy(tile, tt).wait()

                @pl.when(tile + 1 < tiles)
                def _():
                    x_copy(tile + 1, 1 - tt).start()

                @pl.loop(0, upt, step=2)
                def _(q0):
                    for qq in range(2):
                        unit = tile * upt + q0 + qq
                        @pl.when(unit > 0)
                        def _():
                            v_copy(unit - 1, 1 - qq).wait()

                        @pl.when(unit + 1 < tiles * upt)
                        def _():
                            rows_copy(unit + 1, 1 - qq).start()

                        rows_copy(unit, qq).wait()
                        v_copy(unit, qq).start()
                        compute(bufs[qq][0], xbufs[tt][0], unit)

        v_copy(tiles * upt - 1, 1).wait()
        pltpu.sync_copy(hid_v, hid_hbm.at[pl.ds(row0, per)])

    return experts(uv_pack, idx3, xn)


EVAL_TILES = 8


def _pack_bf16_halves(tab):
    half = tab.shape[1] // 2
    bits = lax.bitcast_convert_type(tab.astype(BF16), jnp.uint16).astype(jnp.uint32)
    return bits[:, :half] | (bits[:, half:] << 16)


def _unpack_bf16_halves(w):
    lo = lax.bitcast_convert_type(w << 16, F32)
    hi = lax.bitcast_convert_type(w & jnp.uint32(0xFFFF0000), F32)
    return lo, hi


def _eval_body(vg_ref, hid_ref, g_ref, h_ref, fw_ref, o_ref, *, slots):
    r = SUBLANES
    half = h_ref.shape[1] // 2
    @pl.loop(0, EVAL_TILES)
    def _(tile):
        rows = pl.ds(pl.multiple_of(tile * r, r), r)
        act = _gelu(hid_ref[rows, :]) * g_ref[rows, :]
        acc_lo = jnp.zeros((r, half), F32)
        acc_hi = jnp.zeros((r, half), F32)
        for k in range(slots):
            v_lo, v_hi = _unpack_bf16_halves(vg_ref[tile, k * r:(k + 1) * r, :])
            a = act[:, k:k + 1]
            acc_lo = acc_lo + a * v_lo
            acc_hi = acc_hi + a * v_hi
        o_ref[rows, :] = _rms(h_ref[rows, :] + jnp.concatenate([acc_lo, acc_hi], axis=1), fw_ref[...])


def _peer_eval(vg, hid, gates, h, tok_base, fw, slots):
    t = hid.shape[0]
    d = h.shape[1]
    r = SUBLANES
    g = EVAL_TILES
    s0 = tok_base // (r * g)
    vg = vg.reshape(t // r, slots * r, d // 2)
    return pl.pallas_call(
        functools.partial(_eval_body, slots=slots),
        grid=(t // (r * g),),
        in_specs=[pl.BlockSpec((g, slots * r, d // 2), lambda i: (i, 0, 0)),
                  pl.BlockSpec((g * r, slots), lambda i: (i, 0)), pl.BlockSpec((g * r, slots), lambda i: (i, 0)),
                  pl.BlockSpec((g * r, d), lambda i: (i + s0, 0)), _full(fw.shape)],
        out_specs=pl.BlockSpec((g * r, d), lambda i: (i, 0)),
        out_shape=jax.ShapeDtypeStruct((t, d), F32),
        compiler_params=_params(("parallel",)),
        name="peer_eval",
    )(vg, hid, gates, h, fw)


PEER_CHUNK_TOKENS = 2048
PEER_CHUNKS_AHEAD = 12


def _make_peer(w_q, sub_keys, u_tab, v_tab, final_norm_w):
    d = w_q.shape[0]
    heads, _, nk, dk = sub_keys.shape
    slots = heads * PEER_TOPK
    wqh, wql = _split_bf16(w_q.astype(F32))
    kh, kl = _split_bf16(sub_keys.reshape(heads * 2, nk, dk).astype(F32))
    fw = final_norm_w.reshape(1, d).astype(F32)
    uv_pack = jnp.concatenate([_pack_bf16_halves(u_tab.astype(F32)), _pack_bf16_halves(v_tab.astype(F32))],
                              axis=1)
    r = SUBLANES

    def chunk(xn, h, tok_base, ntok, after):
        expert_t, gates = _peer_route(xn, tok_base, ntok, wqh, wql, kh, kl, heads, nk)
        nb, _, tm = expert_t.shape
        idx = expert_t.reshape(nb, slots, tm // r, r).transpose(0, 2, 1, 3).reshape(-1)
        if after is not None:
            idx, after = lax.optimization_barrier((idx, after))
        hid, vg = _sc_experts(uv_pack, idx, xn, tok_base, slots)
        return _peer_eval(vg, hid.reshape(ntok, slots), gates, h, tok_base, fw, slots), after

    return chunk


def kernel(x, mix_norm_w, w_in, s5_a_re, s5_a_im, s5_log_dt, s5_b_re, s5_b_im, s5_c_re, s5_c_im, s5_d, w_glu, conv_w, conv_b, ssd_dt_bias, ssd_a_log, ssd_d, ssd_norm_w, w_ssd_out, w_o, ffn_norm_w, peer_w_q, peer_sub_keys, peer_u, peer_v, final_norm_w):
    bsz, seqlen, d = x.shape
    mixer = _make_mixer(mix_norm_w[0], w_in[0], s5_a_re[0], s5_a_im[0], s5_log_dt[0], s5_b_re[0], s5_b_im[0],
                        s5_c_re[0], s5_c_im[0], s5_d[0], w_glu[0], conv_w[0], conv_b[0], ssd_dt_bias[0],
                        ssd_a_log[0], ssd_d[0], ssd_norm_w[0], w_ssd_out[0], w_o[0], ffn_norm_w[0])
    peer_chunk = _make_peer(peer_w_q[0], peer_sub_keys[0], peer_u[0], peer_v[0], final_norm_w)
    tc = min(PEER_CHUNK_TOKENS, seqlen)
    outs = []
    for b in range(bsz):
        h, xn = mixer(x[b], 1, seqlen)
        for c in range(seqlen // tc):
            k = len(outs) - PEER_CHUNKS_AHEAD
            out, done = peer_chunk(xn, h, c * tc, tc, outs[k] if k >= 0 else None)
            if k >= 0:
                outs[k] = done
            outs.append(out)
    return jnp.concatenate(outs, axis=0).reshape(bsz, seqlen, d).astype(x.dtype)
```
